```python
import jax, jax.numpy as jnp
from jax import lax
import numpy as np

D_MODEL = 1024
BATCH = 4
SEQ = 8192
DEPTH = 1

CHUNK = 64
Q_BLOCK = 128
A_HEADS = 8
A_HEAD_DIM = 64
A_WIDTH = A_HEADS * A_HEAD_DIM
B_HEADS = 8
B_Q_RANK = 256
B_KV_RANK = 128
B_NOPE = 64
B_ROPE = 32
B_V_DIM = 64
B_WIDTH = B_HEADS * B_V_DIM
ROPE_BASE = 10000.0
FFN_HIDDEN = 2816
NORM_EPS = 1e-6
NEG_INF = -1e30
IN_SIZES = (A_WIDTH, A_WIDTH, A_WIDTH, A_HEADS, B_Q_RANK, B_KV_RANK, B_ROPE, D_MODEL, D_MODEL)
IN_WIDTH = 3 * A_WIDTH + A_HEADS + B_Q_RANK + B_KV_RANK + B_ROPE + 2 * D_MODEL

kernel_name = "hybrid_fox_mla_gated_block"


def rms_norm(x, g):
    xf = x.astype(jnp.float32)
    y = xf * lax.rsqrt(jnp.mean(xf * xf, axis=-1, keepdims=True) + NORM_EPS)
    return (y * g.astype(jnp.float32)).astype(x.dtype)


def rope_cos_sin(positions):
    half = B_ROPE // 2
    inv_freq = ROPE_BASE ** (-jnp.arange(half, dtype=jnp.float32) / half)
    ang = positions.astype(jnp.float32)[..., None] * inv_freq
    return jnp.cos(ang), jnp.sin(ang)


def apply_rope(x, cos, sin):
    xf = x.astype(jnp.float32)
    x1, x2 = jnp.split(xf, 2, axis=-1)
    out = jnp.concatenate([x1 * cos - x2 * sin, x2 * cos + x1 * sin], axis=-1)
    return out.astype(x.dtype)


def to_blocks(t):
    b, s = t.shape[0], t.shape[1]
    return t.reshape((b, s // Q_BLOCK, Q_BLOCK) + t.shape[2:]).swapaxes(0, 1)


def from_blocks(t):
    t = t.swapaxes(0, 1)
    return t.reshape((t.shape[0], t.shape[1] * t.shape[2]) + t.shape[3:])


def forgetting_attention(q, k, v, log_f):
    s_len = q.shape[1]
    nb = s_len // Q_BLOCK
    scale = A_HEAD_DIM ** -0.5
    c = jnp.cumsum(log_f, axis=1)
    c_keys = c.transpose(0, 2, 1)
    k_pos = jnp.arange(s_len)

    def one_block(args):
        i, q_i, c_i = args
        q_pos = i * Q_BLOCK + jnp.arange(Q_BLOCK)
        s = jnp.einsum('bqhd,bkhd->bhqk', q_i, k, preferred_element_type=jnp.float32) * scale
        bias = c_i.transpose(0, 2, 1)[..., :, None] - c_keys[..., None, :]
        mask = k_pos[None, :] <= q_pos[:, None]
        s = jnp.where(mask, s + bias, NEG_INF)
        p = jax.nn.softmax(s, axis=-1).astype(v.dtype)
        return jnp.einsum('bhqk,bkhd->bqhd', p, v)

    out = lax.map(one_block, (jnp.arange(nb), to_blocks(q), to_blocks(c)))
    return from_blocks(out)


def latent_attention(q_nope, q_rope, k_nope, k_rope, v):
    s_len = q_nope.shape[1]
    nb = s_len // Q_BLOCK
    scale = (B_NOPE + B_ROPE) ** -0.5
    k_chunk = jnp.arange(s_len) // CHUNK

    def one_block(args):
        i, qn, qr = args
        q_chunk = (i * Q_BLOCK + jnp.arange(Q_BLOCK)) // CHUNK
        s = (jnp.einsum('bqhd,bkhd->bhqk', qn, k_nope, preferred_element_type=jnp.float32)
             + jnp.einsum('bqhr,bkr->bhqk', qr, k_rope, preferred_element_type=jnp.float32))
        s = jnp.where(k_chunk[None, :] <= q_chunk[:, None], s * scale, NEG_INF)
        p = jax.nn.softmax(s, axis=-1).astype(v.dtype)
        return jnp.einsum('bhqk,bkhd->bqhd', p, v)

    out = lax.map(one_block, (jnp.arange(nb), to_blocks(q_nope), to_blocks(q_rope)))
    return from_blocks(out)


def hybrid_mixer(xn, cos, sin, w_in, b_forget, q_a_norm_g, w_q_up, kv_a_norm_g, w_kv_up,
                 w_branch_a, w_branch_b, b_gate, w_out):
    bsz, s_len, _ = xn.shape
    splits = [int(v) for v in np.cumsum(IN_SIZES)[:-1]]
    proj = xn @ w_in
    qa, ka, va, fa, cq, ckv, kr, ga, gb = jnp.split(proj, splits, axis=-1)

    qa = qa.reshape(bsz, s_len, A_HEADS, A_HEAD_DIM)
    ka = ka.reshape(bsz, s_len, A_HEADS, A_HEAD_DIM)
    va = va.reshape(bsz, s_len, A_HEADS, A_HEAD_DIM)
    log_f = jax.nn.log_sigmoid((fa + b_forget).astype(jnp.float32))
    ya = forgetting_attention(qa, ka, va, log_f).reshape(bsz, s_len, A_WIDTH)

    qb = (rms_norm(cq, q_a_norm_g) @ w_q_up).reshape(bsz, s_len, B_HEADS, B_NOPE + B_ROPE)
    q_nope, q_rope = qb[..., :B_NOPE], qb[..., B_NOPE:]
    q_rope = apply_rope(q_rope, cos[:, :, None, :], sin[:, :, None, :])
    kv = (rms_norm(ckv, kv_a_norm_g) @ w_kv_up).reshape(bsz, s_len, B_HEADS, B_NOPE + B_V_DIM)
    k_nope, vb = kv[..., :B_NOPE], kv[..., B_NOPE:]
    k_rope = apply_rope(kr, cos, sin)
    yb = latent_attention(q_nope, q_rope, k_nope, k_rope, vb).reshape(bsz, s_len, B_WIDTH)

    gate_a = jax.nn.sigmoid(ga + b_gate[:D_MODEL])
    gate_b = jax.nn.sigmoid(gb + b_gate[D_MODEL:])
    merged = gate_a * (ya @ w_branch_a) + gate_b * (yb @ w_branch_b)
    return merged @ w_out


def swiglu(xn, w_gate, w_up, w_down):
    return (jax.nn.silu(xn @ w_gate) * (xn @ w_up)) @ w_down


def setup_inputs(seed: int = 0) -> dict:
    key = jax.random.key(seed)
    ks = jax.random.split(key, 20)
    f32 = jnp.float32

    def w(k, shape, fan_in):
        return jax.random.normal(k, shape, f32) * (fan_in ** -0.5)

    def gain(k, shape):
        return 1.0 + 0.05 * jax.random.normal(k, shape, f32)

    x = jax.random.normal(ks[0], (BATCH, SEQ, D_MODEL), f32)
    offsets = jax.random.randint(ks[1], (BATCH, 1), 0, 100000, dtype=jnp.int32)
    positions = offsets + jnp.arange(SEQ, dtype=jnp.int32)[None, :]
    return {
        "x": x,
        "positions": positions,
        "norm_mix_g": gain(ks[2], (DEPTH, D_MODEL)),
        "w_in": w(ks[3], (DEPTH, D_MODEL, IN_WIDTH), D_MODEL),
        "b_forget": jax.random.uniform(ks[4], (DEPTH, A_HEADS), f32, 1.0, 5.0),
        "q_a_norm_g": gain(ks[5], (DEPTH, B_Q_RANK)),
        "w_q_up": w(ks[6], (DEPTH, B_Q_RANK, B_HEADS * (B_NOPE + B_ROPE)), B_Q_RANK),
        "kv_a_norm_g": gain(ks[7], (DEPTH, B_KV_RANK)),
        "w_kv_up": w(ks[8], (DEPTH, B_KV_RANK, B_HEADS * (B_NOPE + B_V_DIM)), B_KV_RANK),
        "w_branch_a": w(ks[9], (DEPTH, A_WIDTH, D_MODEL), A_WIDTH),
        "w_branch_b": w(ks[10], (DEPTH, B_WIDTH, D_MODEL), B_WIDTH),
        "b_gate": 0.02 * jax.random.normal(ks[11], (DEPTH, 2 * D_MODEL), f32),
        "w_out": w(ks[12], (DEPTH, D_MODEL, D_MODEL), D_MODEL),
        "norm_ffn_g": gain(ks[13], (DEPTH, D_MODEL)),
        "w_ffn_gate": w(ks[14], (DEPTH, D_MODEL, FFN_HIDDEN), D_MODEL),
        "w_ffn_up": w(ks[15], (DEPTH, D_MODEL, FFN_HIDDEN), D_MODEL),
        "w_ffn_down": w(ks[16], (DEPTH, FFN_HIDDEN, D_MODEL), FFN_HIDDEN),
        "norm_final_g": gain(ks[17], (D_MODEL,)),
    }


def reference(x, positions, norm_mix_g, w_in, b_forget, q_a_norm_g, w_q_up, kv_a_norm_g,
              w_kv_up, w_branch_a, w_branch_b, b_gate, w_out, norm_ffn_g, w_ffn_gate,
              w_ffn_up, w_ffn_down, norm_final_g):
    cos, sin = rope_cos_sin(positions)
    h = x
    for l in range(DEPTH):
        xn = rms_norm(h, norm_mix_g[l])
        h = h + hybrid_mixer(xn, cos, sin, w_in[l], b_forget[l], q_a_norm_g[l], w_q_up[l],
                             kv_a_norm_g[l], w_kv_up[l], w_branch_a[l], w_branch_b[l],
                             b_gate[l], w_out[l])
        hn = rms_norm(h, norm_ffn_g[l])
        h = h + swiglu(hn, w_ffn_gate[l], w_ffn_up[l], w_ffn_down[l])
    return rms_norm(h, norm_final_g)
```

```python
import functools
import math

import jax
import jax.numpy as jnp
import numpy as np
from jax import lax
from jax.experimental import pallas as pl
from jax.experimental.pallas import tpu as pltpu

D_MODEL = 1024
CHUNK = 64
A_HEADS = 8
A_HEAD_DIM = 64
A_WIDTH = A_HEADS * A_HEAD_DIM
B_HEADS = 8
B_Q_RANK = 256
B_KV_RANK = 128
B_NOPE = 64
B_ROPE = 32
B_V_DIM = 64
B_WIDTH = B_HEADS * B_V_DIM
ROPE_BASE = 10000.0
FFN_HIDDEN = 2816
NORM_EPS = 1e-6
MASK_VALUE = -1e30

LANES = 128
LOG2E = 1.4426950408889634
A_QSCALE = (A_HEAD_DIM ** -0.5) * LOG2E
B_QSCALE = ((B_NOPE + B_ROPE) ** -0.5) * LOG2E
HALF_ROPE = B_ROPE // 2
B_HEAD_PAD = LANES

TM_IN = 512
TM_OUT = 256
TQ = 256
TK = 256
FFN_CHUNK = 256
VMEM_LIMIT = 56 * 1024 * 1024

F32 = jnp.float32
BF16 = jnp.bfloat16


def _rms(x, g):
    ms = jnp.mean(x * x, axis=-1, keepdims=True)
    return x * lax.rsqrt(ms + NORM_EPS) * g


def _dot(a, b):
    return jnp.dot(a, b, preferred_element_type=F32)


def _rope(t, cos, sin_signed):
    return t * cos + pltpu.roll(t, LANES // 2, axis=1) * sin_signed


def _inproj_kernel(x_ref, pos_ref, g_ref, wq_ref, wk_ref, wv_ref, wf_ref, wcq_ref,
                   wckv_ref, wkr_ref, bf_ref, invf_ref, sgn_ref, gq_ref, wqup_ref,
                   gkv_ref, wkup_ref, wvup_ref,
                   qa_ref, ka_ref, va_ref, ct_ref, qb_ref, kb_ref, vb_ref, carry_ref):
    tm = x_ref.shape[0]
    xn = _rms(x_ref[...], g_ref[...]).astype(BF16)

    qa_ref[...] = (_dot(xn, wq_ref[...]) * A_QSCALE).astype(BF16)
    ka_ref[...] = _dot(xn, wk_ref[...]).astype(BF16)
    va_ref[...] = _dot(xn, wv_ref[...]).astype(BF16)

    z = _dot(xn, wf_ref[...]).T[:A_HEADS] + bf_ref[...]
    c = jnp.minimum(z, 0.0) - jnp.log1p(jnp.exp(-jnp.abs(z)))
    lane = lax.broadcasted_iota(jnp.int32, c.shape, 1)
    d = 1
    while d < tm:
        c = c + jnp.where(lane >= d, pltpu.roll(c, d, axis=1), 0.0)
        d *= 2

    @pl.when(pl.program_id(1) == 0)
    def _():
        carry_ref[...] = jnp.zeros_like(carry_ref)

    c = c + jnp.tile(carry_ref[...], (1, tm // LANES))
    carry_ref[...] = jnp.broadcast_to(c[:, tm - 1:tm], carry_ref.shape)
    ct_ref[...] = c * LOG2E

    ang = pos_ref[...].astype(F32) * invf_ref[...]
    cos = jnp.cos(ang)
    sin_signed = jnp.sin(ang) * sgn_ref[...]

    cqn = _rms(_dot(xn, wcq_ref[...]), gq_ref[...]).astype(BF16)
    qb = _dot(cqn, wqup_ref[...])
    for h in range(B_HEADS):
        sl = slice(h * B_HEAD_PAD, (h + 1) * B_HEAD_PAD)
        qb_ref[:, sl] = (_rope(qb[:, sl], cos, sin_signed) * B_QSCALE).astype(BF16)

    ckvn = _rms(_dot(xn, wckv_ref[...]), gkv_ref[...]).astype(BF16)
    vb_ref[...] = _dot(ckvn, wvup_ref[...]).astype(BF16)
    kn = _dot(ckvn, wkup_ref[...])
    kr = _rope(_dot(xn, wkr_ref[...]), cos, sin_signed)
    for h in range(B_HEADS):
        sl = slice(h * B_HEAD_PAD, (h + 1) * B_HEAD_PAD)
        kb_ref[:, sl] = (kn[:, sl] + kr).astype(BF16)


def _full(shape):
    return pl.BlockSpec(shape, lambda *_: (0,) * len(shape))


def _inproj(x2, pos2, g, wq, wk, wv, wf, wcq, wckv, wkr, bf, invf, sgn, gq, wqup, gkv,
            wkup, wvup, batch, seq):
    t = x2.shape[0]
    nt = seq // TM_IN
    row = lambda w: pl.BlockSpec((TM_IN, w), lambda b, s: (b * nt + s, 0))
    weights = (g, wq, wk, wv, wf, wcq, wckv, wkr, bf, invf, sgn, gq, wqup, gkv, wkup, wvup)
    return pl.pallas_call(
        _inproj_kernel,
        grid=(batch, nt),
        in_specs=[row(D_MODEL), row(1)] + [_full(w.shape) for w in weights],
        out_specs=[
            row(A_WIDTH), row(A_WIDTH), row(A_WIDTH),
            pl.BlockSpec((None, A_HEADS, TM_IN), lambda b, s: (b, 0, s)),
            row(B_HEADS * B_HEAD_PAD), row(B_HEADS * B_HEAD_PAD), row(B_WIDTH),
        ],
        out_shape=[
            jax.ShapeDtypeStruct((t, A_WIDTH), BF16),
            jax.ShapeDtypeStruct((t, A_WIDTH), BF16),
            jax.ShapeDtypeStruct((t, A_WIDTH), BF16),
            jax.ShapeDtypeStruct((batch, A_HEADS, seq), F32),
            jax.ShapeDtypeStruct((t, B_HEADS * B_HEAD_PAD), BF16),
            jax.ShapeDtypeStruct((t, B_HEADS * B_HEAD_PAD), BF16),
            jax.ShapeDtypeStruct((t, B_WIDTH), BF16),
        ],
        scratch_shapes=[pltpu.VMEM((A_HEADS, LANES), F32)],
        compiler_params=pltpu.CompilerParams(
            dimension_semantics=("arbitrary", "arbitrary"), vmem_limit_bytes=VMEM_LIMIT),
        name="inproj",
    )(x2, pos2, *weights)


def _attn_kernel(*refs, fox):
    if fox:
        q_ref, k_ref, v_ref, ct_ref, o_ref, m_scr, l_scr, acc_scr = refs
    else:
        q_ref, k_ref, v_ref, o_ref, m_scr, l_scr, acc_scr = refs
    qi = pl.program_id(2)
    q = q_ref[...]
    lane = lax.broadcasted_iota(jnp.int32, (TQ, LANES), 1)
    first = lane < LANES // 2
    if fox:
        qh = (jnp.where(first, q, jnp.zeros_like(q)), jnp.where(first, jnp.zeros_like(q), q))
        q0 = pl.multiple_of(qi * TQ, TQ)
        cref = [ct_ref[h:h + 1, pl.ds(q0, LANES)][:, 0:1] for h in range(2)]
    else:
        qh = (q[:, :LANES], q[:, LANES:])

    m_scr[...] = jnp.full_like(m_scr, MASK_VALUE)
    l_scr[...] = jnp.zeros_like(l_scr)
    acc_scr[...] = jnp.zeros_like(acc_scr)

    rows = lax.broadcasted_iota(jnp.int32, (TQ, TK), 0)
    cols = lax.broadcasted_iota(jnp.int32, (TQ, TK), 1)
    if fox:
        diag_mask = cols <= rows
    else:
        diag_mask = (cols // CHUNK) <= (rows // CHUNK)

    def step(j, masked):
        start = pl.multiple_of(j * TK, TK)
        k = k_ref[pl.ds(start, TK), :]
        v = v_ref[pl.ds(start, TK), :]
        for h in range(2):
            kh = k if fox else k[:, h * LANES:(h + 1) * LANES]
            s = lax.dot_general(qh[h], kh, (((1,), (1,)), ((), ())),
                                preferred_element_type=F32)
            if fox:
                s = s + (cref[h] - ct_ref[h:h + 1, pl.ds(start, TK)])
            if masked:
                s = jnp.where(diag_mask, s, MASK_VALUE)
            m_prev = m_scr[h]
            m_new = jnp.maximum(m_prev, jnp.max(s, axis=1, keepdims=True))
            alpha = jnp.exp2(m_prev - m_new)
            p = jnp.exp2(s - jnp.tile(m_new, (1, TK // LANES)))
            l_scr[h] = alpha * l_scr[h] + jnp.sum(p, axis=1, keepdims=True)
            acc_scr[h] = alpha * acc_scr[h] + _dot(p.astype(BF16), v)
            m_scr[h] = m_new

    def body(j, carry):
        step(j, False)
        return carry

    lax.fori_loop(0, qi, body, 0)
    step(qi, True)

    o0 = acc_scr[0] / l_scr[0]
    o1 = acc_scr[1] / l_scr[1]
    o_ref[...] = jnp.where(first, o0, o1).astype(o_ref.dtype)


def _attention(q, k, v, ct, batch, seq, fox):
    t = q.shape[0]
    nq = seq // TQ
    qw = LANES if fox else 2 * LANES
    in_specs = [
        pl.BlockSpec((TQ, qw), lambda b, p, i: (b * nq + i, p)),
        pl.BlockSpec((seq, qw), lambda b, p, i: (b, p)),
        pl.BlockSpec((seq, LANES), lambda b, p, i: (b, p)),
    ]
    args = [q, k, v]
    if fox:
        in_specs.append(pl.BlockSpec((None, None, 2, seq), lambda b, p, i: (b, p, 0, 0)))
        args.append(ct)
    return pl.pallas_call(
        functools.partial(_attn_kernel, fox=fox),
        grid=(batch, A_HEADS // 2, nq),
        in_specs=in_specs,
        out_specs=pl.BlockSpec((TQ, LANES), lambda b, p, i: (b * nq + i, p)),
        out_shape=jax.ShapeDtypeStruct((t, A_WIDTH), BF16),
        scratch_shapes=[pltpu.VMEM((2, TQ, LANES), F32)] * 3,
        compiler_params=pltpu.CompilerParams(
            dimension_semantics=("arbitrary", "arbitrary", "arbitrary"),
            vmem_limit_bytes=VMEM_LIMIT),
        name="fox_attention" if fox else "mla_attention",
    )(*args)


def _post_kernel(x_ref, ya_ref, yb_ref, g_ref, wga_ref, wgb_ref, bga_ref, bgb_ref, wa_ref,
                 wb_ref, wout_ref, gffn_ref, wgate_ref, wup_ref, wdown_ref, gfin_ref,
                 o_ref, act_scr):
    x = x_ref[...]
    xn = _rms(x, g_ref[...]).astype(BF16)
    gate_a = jax.nn.sigmoid(_dot(xn, wga_ref[...]) + bga_ref[...])
    gate_b = jax.nn.sigmoid(_dot(xn, wgb_ref[...]) + bgb_ref[...])
    merged = gate_a * _dot(ya_ref[...], wa_ref[...]) + gate_b * _dot(yb_ref[...], wb_ref[...])
    h = x + _dot(merged.astype(BF16), wout_ref[...])
    hn = _rms(h, gffn_ref[...]).astype(BF16)
    for c in range(0, FFN_HIDDEN, FFN_CHUNK):
        sl = slice(c, c + FFN_CHUNK)
        gt = _dot(hn, wgate_ref[:, sl])
        act_scr[:, sl] = (gt * jax.nn.sigmoid(gt) * _dot(hn, wup_ref[:, sl])).astype(BF16)
    h = h + _dot(act_scr[...], wdown_ref[...])
    o_ref[...] = _rms(h, gfin_ref[...])


def _post(x2, ya, yb, *weights):
    t = x2.shape[0]
    row = lambda w: pl.BlockSpec((TM_OUT, w), lambda i: (i, 0))
    const = lambda w: pl.BlockSpec(w.shape, lambda i: (0,) * w.ndim,
                                   pipeline_mode=pl.Buffered(1))
    return pl.pallas_call(
        _post_kernel,
        grid=(t // TM_OUT,),
        in_specs=[row(D_MODEL), row(A_WIDTH), row(B_WIDTH)] + [const(w) for w in weights],
        out_specs=row(D_MODEL),
        out_shape=jax.ShapeDtypeStruct((t, D_MODEL), F32),
        scratch_shapes=[pltpu.VMEM((TM_OUT, FFN_HIDDEN), BF16)],
        compiler_params=pltpu.CompilerParams(
            dimension_semantics=("arbitrary",), vmem_limit_bytes=VMEM_LIMIT),
        name="merge_ffn",
    )(x2, ya, yb, *weights)


def _mla_lane_layout():
    nope = np.concatenate([np.arange(HALF_ROPE, LANES // 2),
                           np.arange(LANES // 2 + HALF_ROPE, LANES // 2 + HALF_ROPE + B_NOPE
                                     - (LANES // 2 - HALF_ROPE))])
    rope = np.concatenate([np.arange(HALF_ROPE), LANES // 2 + np.arange(HALF_ROPE)])
    return nope, rope


def kernel(x, positions, norm_mix_g, w_in, b_forget, q_a_norm_g, w_q_up, kv_a_norm_g, w_kv_up,
           w_branch_a, w_branch_b, b_gate, w_out, norm_ffn_g, w_ffn_gate, w_ffn_up, w_ffn_down,
           norm_final_g):
    batch, seq, _ = x.shape
    t = batch * seq
    assert norm_mix_g.shape[0] == 1 and seq % TM_IN == 0 and seq % TQ == 0 and t % TM_OUT == 0
    x2 = x.reshape(t, D_MODEL)
    pos2 = positions.reshape(t, 1)
    nope_lane, rope_lane = _mla_lane_layout()

    w = w_in[0].astype(BF16)
    o = np.cumsum((0, A_WIDTH, A_WIDTH, A_WIDTH, A_HEADS, B_Q_RANK, B_KV_RANK, B_ROPE,
                   D_MODEL, D_MODEL))
    wq, wk, wv = (w[:, o[i]:o[i + 1]] for i in range(3))
    wf = jnp.zeros((D_MODEL, LANES), BF16).at[:, :A_HEADS].set(w[:, o[3]:o[4]])
    wcq, wckv = w[:, o[4]:o[5]], w[:, o[5]:o[6]]
    wkr = jnp.zeros((D_MODEL, LANES), BF16).at[:, rope_lane].set(w[:, o[6]:o[7]])
    wga, wgb = w[:, o[7]:o[8]], w[:, o[8]:o[9]]

    wqu = w_q_up[0].astype(BF16).reshape(B_Q_RANK, B_HEADS, B_NOPE + B_ROPE)
    wqup = jnp.zeros((B_Q_RANK, B_HEADS, B_HEAD_PAD), BF16)
    wqup = wqup.at[:, :, nope_lane].set(wqu[:, :, :B_NOPE]).at[:, :, rope_lane].set(wqu[:, :, B_NOPE:])
    wqup = wqup.reshape(B_Q_RANK, B_HEADS * B_HEAD_PAD)
    wkvu = w_kv_up[0].astype(BF16).reshape(B_KV_RANK, B_HEADS, B_NOPE + B_V_DIM)
    wkup = jnp.zeros((B_KV_RANK, B_HEADS, B_HEAD_PAD), BF16).at[:, :, nope_lane].set(wkvu[:, :, :B_NOPE])
    wkup = wkup.reshape(B_KV_RANK, B_HEADS * B_HEAD_PAD)
    wvup = wkvu[:, :, B_NOPE:].reshape(B_KV_RANK, B_WIDTH)

    inv_freq = ROPE_BASE ** (-jnp.arange(HALF_ROPE, dtype=F32) / HALF_ROPE)
    invf = jnp.zeros((1, LANES), F32).at[0, rope_lane].set(jnp.tile(inv_freq, 2))
    sgn = jnp.zeros((1, LANES), F32).at[0, rope_lane].set(
        jnp.concatenate([-jnp.ones(HALF_ROPE, F32), jnp.ones(HALF_ROPE, F32)]))
    bf = jnp.broadcast_to(b_forget[0].astype(F32)[:, None], (A_HEADS, TM_IN))
    row = lambda v: v.astype(F32).reshape(1, -1)

    qa, ka, va, ct, qb, kb, vb = _inproj(
        x2, pos2, row(norm_mix_g[0]), wq, wk, wv, wf, wcq, wckv, wkr, bf, invf, sgn,
        row(q_a_norm_g[0]), wqup, row(kv_a_norm_g[0]), wkup, wvup, batch, seq)

    ya = _attention(qa, ka, va, ct.reshape(batch, A_HEADS // 2, 2, seq), batch, seq, fox=True)
    yb = _attention(qb, kb, vb, None, batch, seq, fox=False)

    out = _post(
        x2, ya, yb, row(norm_mix_g[0]), wga, wgb, row(b_gate[0, :D_MODEL]), row(b_gate[0, D_MODEL:]),
        w_branch_a[0].astype(BF16), w_branch_b[0].astype(BF16), w_out[0].astype(BF16),
        row(norm_ffn_g[0]), w_ffn_gate[0].astype(BF16), w_ffn_up[0].astype(BF16),
        w_ffn_down[0].astype(BF16), row(norm_final_g))
    return out.reshape(batch, seq, D_MODEL)
```

```python
import functools
import math

import jax
import jax.numpy as jnp
import numpy as np
from jax import lax
from jax.experimental import pallas as pl
from jax.experimental.pallas import tpu as pltpu

D_MODEL = 1024
CHUNK = 64
A_HEADS = 8
A_HEAD_DIM = 64
A_WIDTH = A_HEADS * A_HEAD_DIM
B_HEADS = 8
B_Q_RANK = 256
B_KV_RANK = 128
B_NOPE = 64
B_ROPE = 32
B_V_DIM = 64
B_WIDTH = B_HEADS * B_V_DIM
ROPE_BASE = 10000.0
FFN_HIDDEN = 2816
NORM_EPS = 1e-6
MASK_VALUE = -1e30

LANES = 128
LOG2E = 1.4426950408889634
A_QSCALE = (A_HEAD_DIM ** -0.5) * LOG2E
B_QSCALE = ((B_NOPE + B_ROPE) ** -0.5) * LOG2E
HALF_ROPE = B_ROPE // 2
B_HEAD_PAD = LANES

TM_IN = 512
TM_OUT = 256
TQ = 256
TK = 1024
FFN_CHUNK = 256
VMEM_LIMIT = 56 * 1024 * 1024

F32 = jnp.float32
BF16 = jnp.bfloat16


def _rms(x, g):
    ms = jnp.mean(x * x, axis=-1, keepdims=True)
    return x * lax.rsqrt(ms + NORM_EPS) * g


def _dot(a, b):
    return jnp.dot(a, b, preferred_element_type=F32)


def _rope(t, cos, sin_signed):
    return t * cos + pltpu.roll(t, LANES // 2, axis=1) * sin_signed


def _inproj_kernel(x_ref, pos_ref, g_ref, wq_ref, wk_ref, wv_ref, wf_ref, wcq_ref,
                   wckv_ref, wkr_ref, bf_ref, invf_ref, sgn_ref, gq_ref, wqup_ref,
                   gkv_ref, wkup_ref, wvup_ref,
                   qa_ref, ka_ref, va_ref, ct_ref, qb_ref, kb_ref, vb_ref, carry_ref):
    tm = x_ref.shape[0]
    xn = _rms(x_ref[...], g_ref[...]).astype(BF16)

    qa_ref[...] = (_dot(xn, wq_ref[...]) * A_QSCALE).astype(BF16)
    ka_ref[...] = _dot(xn, wk_ref[...]).astype(BF16)
    va_ref[...] = _dot(xn, wv_ref[...]).astype(BF16)

    z = _dot(xn, wf_ref[...]).T[:A_HEADS] + bf_ref[...]
    c = jnp.minimum(z, 0.0) - jnp.log1p(jnp.exp(-jnp.abs(z)))
    lane = lax.broadcasted_iota(jnp.int32, c.shape, 1)
    d = 1
    while d < tm:
        c = c + jnp.where(lane >= d, pltpu.roll(c, d, axis=1), 0.0)
        d *= 2

    @pl.when(pl.program_id(1) == 0)
    def _():
        carry_ref[...] = jnp.zeros_like(carry_ref)

    c = c + jnp.tile(carry_ref[...], (1, tm // LANES))
    carry_ref[...] = jnp.broadcast_to(c[:, tm - 1:tm], carry_ref.shape)
    ct_ref[...] = c * LOG2E

    ang = pos_ref[...].astype(F32) * invf_ref[...]
    cos = jnp.cos(ang)
    sin_signed = jnp.sin(ang) * sgn_ref[...]

    cqn = _rms(_dot(xn, wcq_ref[...]), gq_ref[...]).astype(BF16)
    qb = _dot(cqn, wqup_ref[...])
    for h in range(B_HEADS):
        sl = slice(h * B_HEAD_PAD, (h + 1) * B_HEAD_PAD)
        qb_ref[:, sl] = (_rope(qb[:, sl], cos, sin_signed) * B_QSCALE).astype(BF16)

    ckvn = _rms(_dot(xn, wckv_ref[...]), gkv_ref[...]).astype(BF16)
    vb_ref[...] = _dot(ckvn, wvup_ref[...]).astype(BF16)
    kn = _dot(ckvn, wkup_ref[...])
    kr = _rope(_dot(xn, wkr_ref[...]), cos, sin_signed)
    for h in range(B_HEADS):
        sl = slice(h * B_HEAD_PAD, (h + 1) * B_HEAD_PAD)
        kb_ref[:, sl] = (kn[:, sl] + kr).astype(BF16)


def _full(shape):
    return pl.BlockSpec(shape, lambda *_: (0,) * len(shape))


def _inproj(x2, pos2, g, wq, wk, wv, wf, wcq, wckv, wkr, bf, invf, sgn, gq, wqup, gkv,
            wkup, wvup, batch, seq):
    t = x2.shape[0]
    nt = seq // TM_IN
    row = lambda w: pl.BlockSpec((TM_IN, w), lambda b, s: (b * nt + s, 0))
    weights = (g, wq, wk, wv, wf, wcq, wckv, wkr, bf, invf, sgn, gq, wqup, gkv, wkup, wvup)
    return pl.pallas_call(
        _inproj_kernel,
        grid=(batch, nt),
        in_specs=[row(D_MODEL), row(1)] + [_full(w.shape) for w in weights],
        out_specs=[
            row(A_WIDTH), row(A_WIDTH), row(A_WIDTH),
            pl.BlockSpec((None, A_HEADS, TM_IN), lambda b, s: (b, 0, s)),
            row(B_HEADS * B_HEAD_PAD), row(B_HEADS * B_HEAD_PAD), row(B_WIDTH),
        ],
        out_shape=[
            jax.ShapeDtypeStruct((t, A_WIDTH), BF16),
            jax.ShapeDtypeStruct((t, A_WIDTH), BF16),
            jax.ShapeDtypeStruct((t, A_WIDTH), BF16),
            jax.ShapeDtypeStruct((batch, A_HEADS, seq), F32),
            jax.ShapeDtypeStruct((t, B_HEADS * B_HEAD_PAD), BF16),
            jax.ShapeDtypeStruct((t, B_HEADS * B_HEAD_PAD), BF16),
            jax.ShapeDtypeStruct((t, B_WIDTH), BF16),
        ],
        scratch_shapes=[pltpu.VMEM((A_HEADS, LANES), F32)],
        compiler_params=pltpu.CompilerParams(
            dimension_semantics=("arbitrary", "arbitrary"), vmem_limit_bytes=VMEM_LIMIT),
        name="inproj",
    )(x2, pos2, *weights)


def _attn_kernel(*refs, fox):
    if fox:
        q_ref, k_ref, v_ref, ct_ref, o_ref, m_scr, acc_scr, s_scr = refs
    else:
        q_ref, k_ref, v_ref, o_ref, m_scr, acc_scr, s_scr = refs
    qi = pl.program_id(2)
    q = q_ref[...]
    lane = lax.broadcasted_iota(jnp.int32, (TQ, LANES), 1)
    first = lane < LANES // 2
    if fox:
        qh = (jnp.where(first, q, jnp.zeros_like(q)), jnp.where(first, jnp.zeros_like(q), q))
        q0 = pl.multiple_of(qi * TQ, TQ)
        cref = [ct_ref[h:h + 1, pl.ds(q0, LANES)][:, 0:1] for h in range(2)]
    else:
        qh = (q[:, :LANES], q[:, LANES:])

    m_scr[...] = jnp.full_like(m_scr, MASK_VALUE)
    acc_scr[...] = jnp.zeros_like(acc_scr)

    tiles_per_chunk = TK // TQ
    n_full = qi // tiles_per_chunk
    row_off = (qi - n_full * tiles_per_chunk) * TQ
    rows = lax.broadcasted_iota(jnp.int32, (TQ, TK), 0)
    cols = lax.broadcasted_iota(jnp.int32, (TQ, TK), 1)
    if fox:
        last_mask = (cols - rows) <= row_off
    else:
        last_mask = (cols // CHUNK - rows // CHUNK) <= row_off // CHUNK
    ones = jnp.ones((TK, LANES), BF16)

    def scores(j, h):
        start = pl.multiple_of(j * TK, TK)
        kh = k_ref[pl.ds(start, TK), :] if fox else k_ref[pl.ds(start, TK), h * LANES:(h + 1) * LANES]
        s = lax.dot_general(qh[h], kh, (((1,), (1,)), ((), ())),
                            preferred_element_type=F32)
        if fox:
            s = s + (cref[h] - ct_ref[h:h + 1, pl.ds(start, TK)])
        return s

    def update(j, slot, masked):
        start = pl.multiple_of(j * TK, TK)
        v_ones = jnp.concatenate([v_ref[pl.ds(start, TK), :], ones], axis=1)
        for h in range(2):
            s = s_scr[slot, h]
            if masked:
                s = jnp.where(last_mask, s, MASK_VALUE)
            m_prev = m_scr[h]
            m_new = jnp.maximum(m_prev, jnp.max(s, axis=1, keepdims=True))
            alpha = jnp.exp2(m_prev - m_new)
            p = jnp.exp2(s - jnp.tile(m_new, (1, TK // LANES)))
            acc_scr[h] = jnp.tile(alpha, (1, 2)) * acc_scr[h] + _dot(p.astype(BF16), v_ones)
            m_scr[h] = m_new

    def stage(j, slot):
        for h in range(2):
            s_scr[1 - slot, h] = scores(j + 1, h)
        update(j, slot, False)

    for h in range(2):
        s_scr[0, h] = scores(0, h)

    def body(i, carry):
        stage(2 * i, 0)
        stage(2 * i + 1, 1)
        return carry

    lax.fori_loop(0, n_full // 2, body, 0)

    @pl.when(n_full % 2 == 1)
    def _():
        stage(n_full - 1, 0)
        update(n_full, 1, True)

    @pl.when(n_full % 2 == 0)
    def _():
        update(n_full, 0, True)

    o0 = acc_scr[0, :, :LANES] / acc_scr[0, :, LANES:]
    o1 = acc_scr[1, :, :LANES] / acc_scr[1, :, LANES:]
    o_ref[...] = jnp.where(first, o0, o1).astype(o_ref.dtype)


def _attention(q, k, v, ct, batch, seq, fox):
    t = q.shape[0]
    nq = seq // TQ
    qw = LANES if fox else 2 * LANES
    in_specs = [
        pl.BlockSpec((TQ, qw), lambda b, p, i: (b * nq + i, p)),
        pl.BlockSpec((seq, qw), lambda b, p, i: (b, p)),
        pl.BlockSpec((seq, LANES), lambda b, p, i: (b, p)),
    ]
    args = [q, k, v]
    if fox:
        in_specs.append(pl.BlockSpec((None, None, 2, seq), lambda b, p, i: (b, p, 0, 0)))
        args.append(ct)
    return pl.pallas_call(
        functools.partial(_attn_kernel, fox=fox),
        grid=(batch, A_HEADS // 2, nq),
        in_specs=in_specs,
        out_specs=pl.BlockSpec((TQ, LANES), lambda b, p, i: (b * nq + i, p)),
        out_shape=jax.ShapeDtypeStruct((t, A_WIDTH), BF16),
        scratch_shapes=[pltpu.VMEM((2, TQ, LANES), F32), pltpu.VMEM((2, TQ, 2 * LANES), F32),
                        pltpu.VMEM((2, 2, TQ, TK), F32)],
        compiler_params=pltpu.CompilerParams(
            dimension_semantics=("arbitrary", "arbitrary", "arbitrary"),
            vmem_limit_bytes=VMEM_LIMIT),
        name="fox_attention" if fox else "mla_attention",
    )(*args)


def _post_kernel(x_ref, ya_ref, yb_ref, g_ref, wga_ref, wgb_ref, bga_ref, bgb_ref, wa_ref,
                 wb_ref, wout_ref, gffn_ref, wgate_ref, wup_ref, wdown_ref, gfin_ref,
                 o_ref, act_scr):
    x = x_ref[...]
    xn = _rms(x, g_ref[...]).astype(BF16)
    gate_a = jax.nn.sigmoid(_dot(xn, wga_ref[...]) + bga_ref[...])
    gate_b = jax.nn.sigmoid(_dot(xn, wgb_ref[...]) + bgb_ref[...])
    merged = gate_a * _dot(ya_ref[...], wa_ref[...]) + gate_b * _dot(yb_ref[...], wb_ref[...])
    h = x + _dot(merged.astype(BF16), wout_ref[...])
    hn = _rms(h, gffn_ref[...]).astype(BF16)
    for c in range(0, FFN_HIDDEN, FFN_CHUNK):
        sl = slice(c, c + FFN_CHUNK)
        gt = _dot(hn, wgate_ref[:, sl])
        act_scr[:, sl] = (gt * jax.nn.sigmoid(gt) * _dot(hn, wup_ref[:, sl])).astype(BF16)
    h = h + _dot(act_scr[...], wdown_ref[...])
    o_ref[...] = _rms(h, gfin_ref[...])


def _post(x2, ya, yb, *weights):
    t = x2.shape[0]
    row = lambda w: pl.BlockSpec((TM_OUT, w), lambda i: (i, 0))
    const = lambda w: pl.BlockSpec(w.shape, lambda i: (0,) * w.ndim,
                                   pipeline_mode=pl.Buffered(1))
    return pl.pallas_call(
        _post_kernel,
        grid=(t // TM_OUT,),
        in_specs=[row(D_MODEL), row(A_WIDTH), row(B_WIDTH)] + [const(w) for w in weights],
        out_specs=row(D_MODEL),
        out_shape=jax.ShapeDtypeStruct((t, D_MODEL), F32),
        scratch_shapes=[pltpu.VMEM((TM_OUT, FFN_HIDDEN), BF16)],
        compiler_params=pltpu.CompilerParams(
            dimension_semantics=("arbitrary",), vmem_limit_bytes=VMEM_LIMIT),
        name="merge_ffn",
    )(x2, ya, yb, *weights)


def _mla_lane_layout():
    nope = np.concatenate([np.arange(HALF_ROPE, LANES // 2),
                           np.arange(LANES // 2 + HALF_ROPE, LANES // 2 + HALF_ROPE + B_NOPE
                                     - (LANES // 2 - HALF_ROPE))])
    rope = np.concatenate([np.arange(HALF_ROPE), LANES // 2 + np.arange(HALF_ROPE)])
    return nope, rope


def kernel(x, positions, norm_mix_g, w_in, b_forget, q_a_norm_g, w_q_up, kv_a_norm_g, w_kv_up,
           w_branch_a, w_branch_b, b_gate, w_out, norm_ffn_g, w_ffn_gate, w_ffn_up, w_ffn_down,
           norm_final_g):
    batch, seq, _ = x.shape
    t = batch * seq
    assert norm_mix_g.shape[0] == 1 and seq % TM_IN == 0 and seq % TQ == 0 and t % TM_OUT == 0
    x2 = x.reshape(t, D_MODEL)
    pos2 = positions.reshape(t, 1)
    nope_lane, rope_lane = _mla_lane_layout()

    w = w_in[0].astype(BF16)
    o = np.cumsum((0, A_WIDTH, A_WIDTH, A_WIDTH, A_HEADS, B_Q_RANK, B_KV_RANK, B_ROPE,
                   D_MODEL, D_MODEL))
    wq, wk, wv = (w[:, o[i]:o[i + 1]] for i in range(3))
    wf = jnp.zeros((D_MODEL, LANES), BF16).at[:, :A_HEADS].set(w[:, o[3]:o[4]])
    wcq, wckv = w[:, o[4]:o[5]], w[:, o[5]:o[6]]
    wkr = jnp.zeros((D_MODEL, LANES), BF16).at[:, rope_lane].set(w[:, o[6]:o[7]])
    wga, wgb = w[:, o[7]:o[8]], w[:, o[8]:o[9]]

    wqu = w_q_up[0].astype(BF16).reshape(B_Q_RANK, B_HEADS, B_NOPE + B_ROPE)
    wqup = jnp.zeros((B_Q_RANK, B_HEADS, B_HEAD_PAD), BF16)
    wqup = wqup.at[:, :, nope_lane].set(wqu[:, :, :B_NOPE]).at[:, :, rope_lane].set(wqu[:, :, B_NOPE:])
    wqup = wqup.reshape(B_Q_RANK, B_HEADS * B_HEAD_PAD)
    wkvu = w_kv_up[0].astype(BF16).reshape(B_KV_RANK, B_HEADS, B_NOPE + B_V_DIM)
    wkup = jnp.zeros((B_KV_RANK, B_HEADS, B_HEAD_PAD), BF16).at[:, :, nope_lane].set(wkvu[:, :, :B_NOPE])
    wkup = wkup.reshape(B_KV_RANK, B_HEADS * B_HEAD_PAD)
    wvup = wkvu[:, :, B_NOPE:].reshape(B_KV_RANK, B_WIDTH)

    inv_freq = ROPE_BASE ** (-jnp.arange(HALF_ROPE, dtype=F32) / HALF_ROPE)
    invf = jnp.zeros((1, LANES), F32).at[0, rope_lane].set(jnp.tile(inv_freq, 2))
    sgn = jnp.zeros((1, LANES), F32).at[0, rope_lane].set(
        jnp.concatenate([-jnp.ones(HALF_ROPE, F32), jnp.ones(HALF_ROPE, F32)]))
    bf = jnp.broadcast_to(b_forget[0].astype(F32)[:, None], (A_HEADS, TM_IN))
    row = lambda v: v.astype(F32).reshape(1, -1)

    qa, ka, va, ct, qb, kb, vb = _inproj(
        x2, pos2, row(norm_mix_g[0]), wq, wk, wv, wf, wcq, wckv, wkr, bf, invf, sgn,
        row(q_a_norm_g[0]), wqup, row(kv_a_norm_g[0]), wkup, wvup, batch, seq)

    ya = _attention(qa, ka, va, ct.reshape(batch, A_HEADS // 2, 2, seq), batch, seq, fox=True)
    yb = _attention(qb, kb, vb, None, batch, seq, fox=False)

    out = _post(
        x2, ya, yb, row(norm_mix_g[0]), wga, wgb, row(b_gate[0, :D_MODEL]), row(b_gate[0, D_MODEL:]),
        w_branch_a[0].astype(BF16), w_branch_b[0].astype(BF16), w_out[0].astype(BF16),
        row(norm_ffn_g[0]), w_ffn_gate[0].astype(BF16), w_ffn_up[0].astype(BF16),
        w_ffn_down[0].astype(BF16), row(norm_final_g))
    return out.reshape(batch, seq, D_MODEL)
```

```python
import functools
import math

import jax
import jax.numpy as jnp
import numpy as np
from jax import lax
from jax.experimental import pallas as pl
from jax.experimental.pallas import tpu as pltpu

D_MODEL = 1024
CHUNK = 64
A_HEADS = 8
A_HEAD_DIM = 64
A_WIDTH = A_HEADS * A_HEAD_DIM
B_HEADS = 8
B_Q_RANK = 256
B_KV_RANK = 128
B_NOPE = 64
B_ROPE = 32
B_V_DIM = 64
B_WIDTH = B_HEADS * B_V_DIM
ROPE_BASE = 10000.0
FFN_HIDDEN = 2816
NORM_EPS = 1e-6
MASK_VALUE = -1e30

LANES = 128
LOG2E = 1.4426950408889634
A_QSCALE = (A_HEAD_DIM ** -0.5) * LOG2E
B_QSCALE = ((B_NOPE + B_ROPE) ** -0.5) * LOG2E
HALF_ROPE = B_ROPE // 2
B_HEAD_PAD = LANES

TM_IN = 512
TM_OUT = 256
TQ = 512
TK = 512
FFN_CHUNK = 256
VMEM_LIMIT = 56 * 1024 * 1024

F32 = jnp.float32
BF16 = jnp.bfloat16


def _rms(x, g):
    ms = jnp.mean(x * x, axis=-1, keepdims=True)
    return x * lax.rsqrt(ms + NORM_EPS) * g


def _dot(a, b):
    return jnp.dot(a, b, preferred_element_type=F32)


def _rope(t, cos, sin_signed):
    return t * cos + pltpu.roll(t, LANES // 2, axis=1) * sin_signed


def _inproj_kernel(x_ref, pos_ref, g_ref, wq_ref, wk_ref, wv_ref, wf_ref, wcq_ref,
                   wckv_ref, wkr_ref, bf_ref, invf_ref, sgn_ref, gq_ref, wqup_ref,
                   gkv_ref, wkup_ref, wvup_ref,
                   qa_ref, ka_ref, va_ref, ct_ref, qb_ref, kb_ref, vb_ref, carry_ref):
    tm = x_ref.shape[0]
    xn = _rms(x_ref[...], g_ref[...]).astype(BF16)

    qa_ref[...] = (_dot(xn, wq_ref[...]) * A_QSCALE).astype(BF16)
    ka_ref[...] = _dot(xn, wk_ref[...]).astype(BF16)
    va_ref[...] = _dot(xn, wv_ref[...]).astype(BF16)

    z = _dot(xn, wf_ref[...]).T[:A_HEADS] + bf_ref[...]
    c = jnp.minimum(z, 0.0) - jnp.log1p(jnp.exp(-jnp.abs(z)))
    lane = lax.broadcasted_iota(jnp.int32, c.shape, 1)
    d = 1
    while d < tm:
        c = c + jnp.where(lane >= d, pltpu.roll(c, d, axis=1), 0.0)
        d *= 2

    @pl.when(pl.program_id(1) == 0)
    def _():
        carry_ref[...] = jnp.zeros_like(carry_ref)

    c = c + jnp.tile(carry_ref[...], (1, tm // LANES))
    carry_ref[...] = jnp.broadcast_to(c[:, tm - 1:tm], carry_ref.shape)
    ct_ref[...] = c * LOG2E

    ang = pos_ref[...].astype(F32) * invf_ref[...]
    cos = jnp.cos(ang)
    sin_signed = jnp.sin(ang) * sgn_ref[...]

    cqn = _rms(_dot(xn, wcq_ref[...]), gq_ref[...]).astype(BF16)
    qb = _dot(cqn, wqup_ref[...])
    for h in range(B_HEADS):
        sl = slice(h * B_HEAD_PAD, (h + 1) * B_HEAD_PAD)
        qb_ref[:, sl] = (_rope(qb[:, sl], cos, sin_signed) * B_QSCALE).astype(BF16)

    ckvn = _rms(_dot(xn, wckv_ref[...]), gkv_ref[...]).astype(BF16)
    vb_ref[...] = _dot(ckvn, wvup_ref[...]).astype(BF16)
    kn = _dot(ckvn, wkup_ref[...])
    kr = _rope(_dot(xn, wkr_ref[...]), cos, sin_signed)
    for h in range(B_HEADS):
        sl = slice(h * B_HEAD_PAD, (h + 1) * B_HEAD_PAD)
        kb_ref[:, sl] = (kn[:, sl] + kr).astype(BF16)


def _full(shape):
    return pl.BlockSpec(shape, lambda *_: (0,) * len(shape))


def _inproj(x2, pos2, g, wq, wk, wv, wf, wcq, wckv, wkr, bf, invf, sgn, gq, wqup, gkv,
            wkup, wvup, batch, seq):
    t = x2.shape[0]
    nt = seq // TM_IN
    row = lambda w: pl.BlockSpec((TM_IN, w), lambda b, s: (b * nt + s, 0))
    weights = (g, wq, wk, wv, wf, wcq, wckv, wkr, bf, invf, sgn, gq, wqup, gkv, wkup, wvup)
    return pl.pallas_call(
        _inproj_kernel,
        grid=(batch, nt),
        in_specs=[row(D_MODEL), row(1)] + [_full(w.shape) for w in weights],
        out_specs=[
            row(A_WIDTH), row(A_WIDTH), row(A_WIDTH),
            pl.BlockSpec((None, A_HEADS, TM_IN), lambda b, s: (b, 0, s)),
            row(B_HEADS * B_HEAD_PAD), row(B_HEADS * B_HEAD_PAD), row(B_WIDTH),
        ],
        out_shape=[
            jax.ShapeDtypeStruct((t, A_WIDTH), BF16),
            jax.ShapeDtypeStruct((t, A_WIDTH), BF16),
            jax.ShapeDtypeStruct((t, A_WIDTH), BF16),
            jax.ShapeDtypeStruct((batch, A_HEADS, seq), F32),
            jax.ShapeDtypeStruct((t, B_HEADS * B_HEAD_PAD), BF16),
            jax.ShapeDtypeStruct((t, B_HEADS * B_HEAD_PAD), BF16),
            jax.ShapeDtypeStruct((t, B_WIDTH), BF16),
        ],
        scratch_shapes=[pltpu.VMEM((A_HEADS, LANES), F32)],
        compiler_params=pltpu.CompilerParams(
            dimension_semantics=("arbitrary", "arbitrary"), vmem_limit_bytes=VMEM_LIMIT),
        name="inproj",
    )(x2, pos2, *weights)


def _attn_kernel(*refs, fox):
    if fox:
        q_ref, k_ref, v_ref, ct_ref, o_ref, m_scr, acc_scr, s_scr = refs
    else:
        q_ref, k_ref, v_ref, o_ref, m_scr, acc_scr, s_scr = refs
    qi = pl.program_id(2)
    q = q_ref[...]
    lane = lax.broadcasted_iota(jnp.int32, (TQ, LANES), 1)
    first = lane < LANES // 2
    if fox:
        qh = (jnp.where(first, q, jnp.zeros_like(q)), jnp.where(first, jnp.zeros_like(q), q))
        q0 = pl.multiple_of(qi * TQ, TQ)
        cref = [ct_ref[h:h + 1, pl.ds(q0, LANES)][:, 0:1] for h in range(2)]
    else:
        qh = (q[:, :LANES], q[:, LANES:])

    m_scr[...] = jnp.full_like(m_scr, MASK_VALUE)
    acc_scr[...] = jnp.zeros_like(acc_scr)

    tiles_per_chunk = TK // TQ
    n_full = qi // tiles_per_chunk
    row_off = (qi - n_full * tiles_per_chunk) * TQ
    rows = lax.broadcasted_iota(jnp.int32, (TQ, TK), 0)
    cols = lax.broadcasted_iota(jnp.int32, (TQ, TK), 1)
    if fox:
        last_mask = (cols - rows) <= row_off
    else:
        last_mask = (cols // CHUNK - rows // CHUNK) <= row_off // CHUNK
    ones = jnp.ones((TK, LANES), BF16)

    def scores(j, h):
        start = pl.multiple_of(j * TK, TK)
        kh = k_ref[pl.ds(start, TK), :] if fox else k_ref[pl.ds(start, TK), h * LANES:(h + 1) * LANES]
        s = lax.dot_general(qh[h], kh, (((1,), (1,)), ((), ())),
                            preferred_element_type=F32)
        if fox:
            s = s + (cref[h] - ct_ref[h:h + 1, pl.ds(start, TK)])
        return s

    def update(j, slot, masked):
        start = pl.multiple_of(j * TK, TK)
        v_ones = jnp.concatenate([v_ref[pl.ds(start, TK), :], ones], axis=1)
        for h in range(2):
            s = s_scr[slot, h]
            if masked:
                s = jnp.where(last_mask, s, MASK_VALUE)
            m_prev = m_scr[h]
            m_new = jnp.maximum(m_prev, jnp.max(s, axis=1, keepdims=True))
            alpha = jnp.exp2(m_prev - m_new)
            p = jnp.exp2(s - jnp.tile(m_new, (1, TK // LANES)))
            acc_scr[h] = jnp.tile(alpha, (1, 2)) * acc_scr[h] + _dot(p.astype(BF16), v_ones)
            m_scr[h] = m_new

    def stage(j, slot):
        for h in range(2):
            s_scr[1 - slot, h] = scores(j + 1, h)
        update(j, slot, False)

    for h in range(2):
        s_scr[0, h] = scores(0, h)

    def body(i, carry):
        stage(2 * i, 0)
        stage(2 * i + 1, 1)
        return carry

    lax.fori_loop(0, n_full // 2, body, 0)

    @pl.when(n_full % 2 == 1)
    def _():
        stage(n_full - 1, 0)
        update(n_full, 1, True)

    @pl.when(n_full % 2 == 0)
    def _():
        update(n_full, 0, True)

    o0 = acc_scr[0, :, :LANES] / acc_scr[0, :, LANES:]
    o1 = acc_scr[1, :, :LANES] / acc_scr[1, :, LANES:]
    o_ref[...] = jnp.where(first, o0, o1).astype(o_ref.dtype)


def _attention(q, k, v, ct, batch, seq, fox):
    t = q.shape[0]
    nq = seq // TQ
    qw = LANES if fox else 2 * LANES
    in_specs = [
        pl.BlockSpec((TQ, qw), lambda b, p, i: (b * nq + i, p)),
        pl.BlockSpec((seq, qw), lambda b, p, i: (b, p)),
        pl.BlockSpec((seq, LANES), lambda b, p, i: (b, p)),
    ]
    args = [q, k, v]
    if fox:
        in_specs.append(pl.BlockSpec((None, None, 2, seq), lambda b, p, i: (b, p, 0, 0)))
        args.append(ct)
    return pl.pallas_call(
        functools.partial(_attn_kernel, fox=fox),
        grid=(batch, A_HEADS // 2, nq),
        in_specs=in_specs,
        out_specs=pl.BlockSpec((TQ, LANES), lambda b, p, i: (b * nq + i, p)),
        out_shape=jax.ShapeDtypeStruct((t, A_WIDTH), BF16),
        scratch_shapes=[pltpu.VMEM((2, TQ, LANES), F32), pltpu.VMEM((2, TQ, 2 * LANES), F32),
                        pltpu.VMEM((2, 2, TQ, TK), F32)],
        compiler_params=pltpu.CompilerParams(
            dimension_semantics=("arbitrary", "arbitrary", "arbitrary"),
            vmem_limit_bytes=VMEM_LIMIT),
        name="fox_attention" if fox else "mla_attention",
    )(*args)


def _post_kernel(x_ref, ya_ref, yb_ref, g_ref, wga_ref, wgb_ref, bga_ref, bgb_ref, wa_ref,
                 wb_ref, wout_ref, gffn_ref, wgate_ref, wup_ref, wdown_ref, gfin_ref,
                 o_ref, act_scr):
    x = x_ref[...]
    xn = _rms(x, g_ref[...]).astype(BF16)
    gate_a = jax.nn.sigmoid(_dot(xn, wga_ref[...]) + bga_ref[...])
    gate_b = jax.nn.sigmoid(_dot(xn, wgb_ref[...]) + bgb_ref[...])
    merged = gate_a * _dot(ya_ref[...], wa_ref[...]) + gate_b * _dot(yb_ref[...], wb_ref[...])
    h = x + _dot(merged.astype(BF16), wout_ref[...])
    hn = _rms(h, gffn_ref[...]).astype(BF16)
    for c in range(0, FFN_HIDDEN, FFN_CHUNK):
        sl = slice(c, c + FFN_CHUNK)
        gt = _dot(hn, wgate_ref[:, sl])
        act_scr[:, sl] = (gt * jax.nn.sigmoid(gt) * _dot(hn, wup_ref[:, sl])).astype(BF16)
    h = h + _dot(act_scr[...], wdown_ref[...])
    o_ref[...] = _rms(h, gfin_ref[...])


def _post(x2, ya, yb, *weights):
    t = x2.shape[0]
    row = lambda w: pl.BlockSpec((TM_OUT, w), lambda i: (i, 0))
    const = lambda w: pl.BlockSpec(w.shape, lambda i: (0,) * w.ndim,
                                   pipeline_mode=pl.Buffered(1))
    return pl.pallas_call(
        _post_kernel,
        grid=(t // TM_OUT,),
        in_specs=[row(D_MODEL), row(A_WIDTH), row(B_WIDTH)] + [const(w) for w in weights],
        out_specs=row(D_MODEL),
        out_shape=jax.ShapeDtypeStruct((t, D_MODEL), F32),
        scratch_shapes=[pltpu.VMEM((TM_OUT, FFN_HIDDEN), BF16)],
        compiler_params=pltpu.CompilerParams(
            dimension_semantics=("arbitrary",), vmem_limit_bytes=VMEM_LIMIT),
        name="merge_ffn",
    )(x2, ya, yb, *weights)


def _mla_lane_layout():
    nope = np.concatenate([np.arange(HALF_ROPE, LANES // 2),
                           np.arange(LANES // 2 + HALF_ROPE, LANES // 2 + HALF_ROPE + B_NOPE
                                     - (LANES // 2 - HALF_ROPE))])
    rope = np.concatenate([np.arange(HALF_ROPE), LANES // 2 + np.arange(HALF_ROPE)])
    return nope, rope


def kernel(x, positions, norm_mix_g, w_in, b_forget, q_a_norm_g, w_q_up, kv_a_norm_g, w_kv_up,
           w_branch_a, w_branch_b, b_gate, w_out, norm_ffn_g, w_ffn_gate, w_ffn_up, w_ffn_down,
           norm_final_g):
    batch, seq, _ = x.shape
    t = batch * seq
    assert norm_mix_g.shape[0] == 1 and seq % TM_IN == 0 and seq % TQ == 0 and t % TM_OUT == 0
    x2 = x.reshape(t, D_MODEL)
    pos2 = positions.reshape(t, 1)
    nope_lane, rope_lane = _mla_lane_layout()

    w = w_in[0].astype(BF16)
    o = np.cumsum((0, A_WIDTH, A_WIDTH, A_WIDTH, A_HEADS, B_Q_RANK, B_KV_RANK, B_ROPE,
                   D_MODEL, D_MODEL))
    wq, wk, wv = (w[:, o[i]:o[i + 1]] for i in range(3))
    wf = jnp.zeros((D_MODEL, LANES), BF16).at[:, :A_HEADS].set(w[:, o[3]:o[4]])
    wcq, wckv = w[:, o[4]:o[5]], w[:, o[5]:o[6]]
    wkr = jnp.zeros((D_MODEL, LANES), BF16).at[:, rope_lane].set(w[:, o[6]:o[7]])
    wga, wgb = w[:, o[7]:o[8]], w[:, o[8]:o[9]]

    wqu = w_q_up[0].astype(BF16).reshape(B_Q_RANK, B_HEADS, B_NOPE + B_ROPE)
    wqup = jnp.zeros((B_Q_RANK, B_HEADS, B_HEAD_PAD), BF16)
    wqup = wqup.at[:, :, nope_lane].set(wqu[:, :, :B_NOPE]).at[:, :, rope_lane].set(wqu[:, :, B_NOPE:])
    wqup = wqup.reshape(B_Q_RANK, B_HEADS * B_HEAD_PAD)
    wkvu = w_kv_up[0].astype(BF16).reshape(B_KV_RANK, B_HEADS, B_NOPE + B_V_DIM)
    wkup = jnp.zeros((B_KV_RANK, B_HEADS, B_HEAD_PAD), BF16).at[:, :, nope_lane].set(wkvu[:, :, :B_NOPE])
    wkup = wkup.reshape(B_KV_RANK, B_HEADS * B_HEAD_PAD)
    wvup = wkvu[:, :, B_NOPE:].reshape(B_KV_RANK, B_WIDTH)

    inv_freq = ROPE_BASE ** (-jnp.arange(HALF_ROPE, dtype=F32) / HALF_ROPE)
    invf = jnp.zeros((1, LANES), F32).at[0, rope_lane].set(jnp.tile(inv_freq, 2))
    sgn = jnp.zeros((1, LANES), F32).at[0, rope_lane].set(
        jnp.concatenate([-jnp.ones(HALF_ROPE, F32), jnp.ones(HALF_ROPE, F32)]))
    bf = jnp.broadcast_to(b_forget[0].astype(F32)[:, None], (A_HEADS, TM_IN))
    row = lambda v: v.astype(F32).reshape(1, -1)

    qa, ka, va, ct, qb, kb, vb = _inproj(
        x2, pos2, row(norm_mix_g[0]), wq, wk, wv, wf, wcq, wckv, wkr, bf, invf, sgn,
        row(q_a_norm_g[0]), wqup, row(kv_a_norm_g[0]), wkup, wvup, batch, seq)

    ya = _attention(qa, ka, va, ct.reshape(batch, A_HEADS // 2, 2, seq), batch, seq, fox=True)
    yb = _attention(qb, kb, vb, None, batch, seq, fox=False)

    out = _post(
        x2, ya, yb, row(norm_mix_g[0]), wga, wgb, row(b_gate[0, :D_MODEL]), row(b_gate[0, D_MODEL:]),
        w_branch_a[0].astype(BF16), w_branch_b[0].astype(BF16), w_out[0].astype(BF16),
        row(norm_ffn_g[0]), w_ffn_gate[0].astype(BF16), w_ffn_up[0].astype(BF16),
        w_ffn_down[0].astype(BF16), row(norm_final_g))
    return out.reshape(batch, seq, D_MODEL)
```

```python
import functools

import jax
import jax.numpy as jnp
import numpy as np
from jax import lax
from jax.experimental import pallas as pl
from jax.experimental.pallas import tpu as pltpu

D_MODEL = 1024
CHUNK = 64
A_HEADS = 8
A_HEAD_DIM = 64
A_WIDTH = A_HEADS * A_HEAD_DIM
B_HEADS = 8
B_Q_RANK = 256
B_KV_RANK = 128
B_NOPE = 64
B_ROPE = 32
B_V_DIM = 64
B_WIDTH = B_HEADS * B_V_DIM
ROPE_BASE = 10000.0
FFN_HIDDEN = 2816
NORM_EPS = 1e-6
MASK_VALUE = -1e30

LANES = 128
BF16_ROWS = 16
LOG2E = 1.4426950408889634
A_QSCALE = (A_HEAD_DIM ** -0.5) * LOG2E
B_QSCALE = ((B_NOPE + B_ROPE) ** -0.5) * LOG2E
HALF_ROPE = B_ROPE // 2
B_HEAD_PAD = LANES
N_PAIRS = A_HEADS // 2
BIAS_PIECES = 3

TM_IN = 512
TM_OUT = 256
TQ = 512
TK = 512
FFN_CHUNK = 256
VMEM_LIMIT = 56 * 1024 * 1024

F32 = jnp.float32
BF16 = jnp.bfloat16


def _rms(x, g):
    ms = jnp.mean(x * x, axis=-1, keepdims=True)
    return x * lax.rsqrt(ms + NORM_EPS) * g


def _dot(a, b):
    return jnp.dot(a, b, preferred_element_type=F32)


def _rope(t, cos, sin_signed):
    return t * cos + pltpu.roll(t, LANES // 2, axis=1) * sin_signed


def _inproj_kernel(x_ref, pos_ref, g_ref, wq_ref, wk_ref, wv_ref, wf_ref, wcq_ref,
                   wckv_ref, wkr_ref, bf_ref, invf_ref, sgn_ref, gq_ref, wqup_ref,
                   gkv_ref, wkup_ref, wvup_ref,
                   qat_ref, ka_ref, vat_ref, cb_ref, qbt_ref, kb_ref, vbt_ref, carry_ref):
    tm = x_ref.shape[0]
    xn = _rms(x_ref[...], g_ref[...]).astype(BF16)

    def store_transposed(ref, val, groups):
        for i in range(groups):
            ref[i] = val[:, i * LANES:(i + 1) * LANES].T.astype(BF16)

    store_transposed(qat_ref, _dot(xn, wq_ref[...]) * A_QSCALE, N_PAIRS)
    ka_ref[...] = _dot(xn, wk_ref[...]).astype(BF16)
    store_transposed(vat_ref, _dot(xn, wv_ref[...]), N_PAIRS)

    z = _dot(xn, wf_ref[...]).T[:A_HEADS] + bf_ref[...]
    c = jnp.minimum(z, 0.0) - jnp.log1p(jnp.exp(-jnp.abs(z)))
    lane = lax.broadcasted_iota(jnp.int32, c.shape, 1)
    d = 1
    while d < tm:
        c = c + jnp.where(lane >= d, pltpu.roll(c, d, axis=1), 0.0)
        d *= 2

    @pl.when(pl.program_id(1) == 0)
    def _():
        carry_ref[...] = jnp.zeros_like(carry_ref)

    c = c + jnp.tile(carry_ref[...], (1, tm // LANES))
    carry_ref[...] = jnp.broadcast_to(c[:, tm - 1:tm], carry_ref.shape)
    c = c * LOG2E
    hi = c.astype(BF16).astype(F32)
    mid = (c - hi).astype(BF16).astype(F32)
    lo = c - hi - mid
    pad = jnp.zeros((LANES - BIAS_PIECES * A_HEADS, tm), F32)
    cb_ref[...] = jnp.concatenate([hi, mid, lo, pad], axis=0).T.astype(BF16)

    ang = pos_ref[...].astype(F32) * invf_ref[...]
    cos = jnp.cos(ang)
    sin_signed = jnp.sin(ang) * sgn_ref[...]

    cqn = _rms(_dot(xn, wcq_ref[...]), gq_ref[...]).astype(BF16)
    qb = _dot(cqn, wqup_ref[...])
    for h in range(B_HEADS):
        sl = slice(h * B_HEAD_PAD, (h + 1) * B_HEAD_PAD)
        qbt_ref[h] = (_rope(qb[:, sl], cos, sin_signed) * B_QSCALE).T.astype(BF16)

    ckvn = _rms(_dot(xn, wckv_ref[...]), gkv_ref[...]).astype(BF16)
    store_transposed(vbt_ref, _dot(ckvn, wvup_ref[...]), N_PAIRS)
    kn = _dot(ckvn, wkup_ref[...])
    kr = _rope(_dot(xn, wkr_ref[...]), cos, sin_signed)
    for h in range(B_HEADS):
        sl = slice(h * B_HEAD_PAD, (h + 1) * B_HEAD_PAD)
        kb_ref[:, sl] = (kn[:, sl] + kr).astype(BF16)


def _full(shape):
    return pl.BlockSpec(shape, lambda *_: (0,) * len(shape))


def _inproj(x2, pos2, g, wq, wk, wv, wf, wcq, wckv, wkr, bf, invf, sgn, gq, wqup, gkv,
            wkup, wvup, batch, seq):
    t = x2.shape[0]
    nt = seq // TM_IN
    row = lambda w: pl.BlockSpec((TM_IN, w), lambda b, s: (b * nt + s, 0))
    col = lambda n: pl.BlockSpec((None, n, LANES, TM_IN), lambda b, s: (b, 0, 0, s))
    col_shape = lambda n: jax.ShapeDtypeStruct((batch, n, LANES, seq), BF16)
    weights = (g, wq, wk, wv, wf, wcq, wckv, wkr, bf, invf, sgn, gq, wqup, gkv, wkup, wvup)
    return pl.pallas_call(
        _inproj_kernel,
        grid=(batch, nt),
        in_specs=[row(D_MODEL), row(1)] + [_full(w.shape) for w in weights],
        out_specs=[
            col(N_PAIRS), row(A_WIDTH), col(N_PAIRS), row(LANES),
            col(B_HEADS), row(B_HEADS * B_HEAD_PAD), col(N_PAIRS),
        ],
        out_shape=[
            col_shape(N_PAIRS),
            jax.ShapeDtypeStruct((t, A_WIDTH), BF16),
            col_shape(N_PAIRS),
            jax.ShapeDtypeStruct((t, LANES), BF16),
            col_shape(B_HEADS),
            jax.ShapeDtypeStruct((t, B_HEADS * B_HEAD_PAD), BF16),
            col_shape(N_PAIRS),
        ],
        scratch_shapes=[pltpu.VMEM((A_HEADS, LANES), F32)],
        compiler_params=pltpu.CompilerParams(
            dimension_semantics=("arbitrary", "arbitrary"), vmem_limit_bytes=VMEM_LIMIT),
        name="inproj",
    )(x2, pos2, *weights)


def _attn_kernel(*refs, fox):
    if fox:
        qt_ref, k_ref, vt_ref, cb_ref, o_ref, m_scr, acc_scr, s_scr = refs
    else:
        qt_ref, k_ref, vt_ref, o_ref, m_scr, acc_scr, s_scr = refs
    qi = pl.program_id(2)
    head_dim = LANES // 2
    if fox:
        qt = qt_ref[...]
        frow = lax.broadcasted_iota(jnp.int32, (LANES, TQ), 0)
        pair = pl.program_id(1)
        rhs = []
        for h in range(2):
            mine = (frow < head_dim) == (h == 0)
            picks = (frow % A_HEADS == 2 * pair + h) & (frow < BIAS_PIECES * A_HEADS)
            rhs.append(jnp.concatenate(
                [jnp.where(mine, qt, jnp.zeros_like(qt)),
                 jnp.where(picks, -1.0, 0.0).astype(BF16)], axis=0))
    else:
        rhs = [qt_ref[0], qt_ref[1]]

    m_scr[...] = jnp.full_like(m_scr, MASK_VALUE)
    acc_scr[...] = jnp.zeros_like(acc_scr)

    keys = lax.broadcasted_iota(jnp.int32, (TK, TQ), 0)
    queries = lax.broadcasted_iota(jnp.int32, (TK, TQ), 1)
    if fox:
        diag_mask = keys <= queries
    else:
        diag_mask = (keys // CHUNK) <= (queries // CHUNK)
    ones = jnp.ones((BF16_ROWS, TK), BF16)

    def scores(j, h):
        start = pl.multiple_of(j * TK, TK)
        if fox:
            lhs = jnp.concatenate([k_ref[pl.ds(start, TK), :], cb_ref[pl.ds(start, TK), :]], axis=1)
        else:
            lhs = k_ref[pl.ds(start, TK), h * LANES:(h + 1) * LANES]
        return _dot(lhs, rhs[h])

    def update(j, slot, masked):
        start = pl.multiple_of(j * TK, TK)
        for h in range(2):
            s = s_scr[slot, h]
            if masked:
                s = jnp.where(diag_mask, s, MASK_VALUE)
            m_prev = m_scr[h]
            m_new = jnp.maximum(m_prev, jnp.max(s, axis=0, keepdims=True))
            alpha = jnp.exp2(m_prev - m_new)
            p = jnp.exp2(s - m_new).astype(BF16)
            vt = jnp.concatenate(
                [vt_ref[h * head_dim:(h + 1) * head_dim, pl.ds(start, TK)], ones], axis=0)
            acc_scr[h] = alpha * acc_scr[h] + _dot(vt, p)
            m_scr[h] = m_new

    def stage(j, slot):
        for h in range(2):
            s_scr[1 - slot, h] = scores(j + 1, h)
        update(j, slot, False)

    for h in range(2):
        s_scr[0, h] = scores(0, h)

    def body(i, carry):
        stage(2 * i, 0)
        stage(2 * i + 1, 1)
        return carry

    lax.fori_loop(0, qi // 2, body, 0)

    @pl.when(qi % 2 == 1)
    def _():
        stage(qi - 1, 0)
        update(qi, 1, True)

    @pl.when(qi % 2 == 0)
    def _():
        update(qi, 0, True)

    out_t = jnp.concatenate(
        [acc_scr[h, :head_dim] / acc_scr[h, head_dim:head_dim + 1] for h in range(2)], axis=0)
    o_ref[...] = out_t.T.astype(o_ref.dtype)


def _attention(qt, k, vt, cb, batch, seq, fox):
    t = k.shape[0]
    nq = seq // TQ
    assert TQ == TK
    if fox:
        qt_spec = pl.BlockSpec((None, None, LANES, TQ), lambda b, p, i: (b, p, 0, i))
    else:
        qt_spec = pl.BlockSpec((None, 2, LANES, TQ), lambda b, p, i: (b, p, 0, i))
    in_specs = [
        qt_spec,
        pl.BlockSpec((seq, LANES if fox else 2 * LANES), lambda b, p, i: (b, p)),
        pl.BlockSpec((None, None, LANES, seq), lambda b, p, i: (b, p, 0, 0)),
    ]
    args = [qt, k, vt]
    if fox:
        in_specs.append(pl.BlockSpec((seq, LANES), lambda b, p, i: (b, 0)))
        args.append(cb)
    acc_rows = LANES // 2 + BF16_ROWS
    return pl.pallas_call(
        functools.partial(_attn_kernel, fox=fox),
        grid=(batch, N_PAIRS, nq),
        in_specs=in_specs,
        out_specs=pl.BlockSpec((TQ, LANES), lambda b, p, i: (b * nq + i, p)),
        out_shape=jax.ShapeDtypeStruct((t, A_WIDTH), BF16),
        scratch_shapes=[pltpu.VMEM((2, 1, TQ), F32), pltpu.VMEM((2, acc_rows, TQ), F32),
                        pltpu.VMEM((2, 2, TK, TQ), F32)],
        compiler_params=pltpu.CompilerParams(
            dimension_semantics=("arbitrary", "arbitrary", "arbitrary"),
            vmem_limit_bytes=VMEM_LIMIT),
        name="fox_attention" if fox else "mla_attention",
    )(*args)


def _post_kernel(x_ref, ya_ref, yb_ref, g_ref, wga_ref, wgb_ref, bga_ref, bgb_ref, wa_ref,
                 wb_ref, wout_ref, gffn_ref, wgate_ref, wup_ref, wdown_ref, gfin_ref,
                 o_ref, act_scr):
    x = x_ref[...]
    xn = _rms(x, g_ref[...]).astype(BF16)
    gate_a = jax.nn.sigmoid(_dot(xn, wga_ref[...]) + bga_ref[...])
    gate_b = jax.nn.sigmoid(_dot(xn, wgb_ref[...]) + bgb_ref[...])
    merged = gate_a * _dot(ya_ref[...], wa_ref[...]) + gate_b * _dot(yb_ref[...], wb_ref[...])
    h = x + _dot(merged.astype(BF16), wout_ref[...])
    hn = _rms(h, gffn_ref[...]).astype(BF16)
    for c in range(0, FFN_HIDDEN, FFN_CHUNK):
        sl = slice(c, c + FFN_CHUNK)
        gt = _dot(hn, wgate_ref[:, sl])
        act_scr[:, sl] = (gt * jax.nn.sigmoid(gt) * _dot(hn, wup_ref[:, sl])).astype(BF16)
    h = h + _dot(act_scr[...], wdown_ref[...])
    o_ref[...] = _rms(h, gfin_ref[...])


def _post(x2, ya, yb, *weights):
    t = x2.shape[0]
    row = lambda w: pl.BlockSpec((TM_OUT, w), lambda i: (i, 0))
    const = lambda w: pl.BlockSpec(w.shape, lambda i: (0,) * w.ndim,
                                   pipeline_mode=pl.Buffered(1))
    return pl.pallas_call(
        _post_kernel,
        grid=(t // TM_OUT,),
        in_specs=[row(D_MODEL), row(A_WIDTH), row(B_WIDTH)] + [const(w) for w in weights],
        out_specs=row(D_MODEL),
        out_shape=jax.ShapeDtypeStruct((t, D_MODEL), F32),
        scratch_shapes=[pltpu.VMEM((TM_OUT, FFN_HIDDEN), BF16)],
        compiler_params=pltpu.CompilerParams(
            dimension_semantics=("arbitrary",), vmem_limit_bytes=VMEM_LIMIT),
        name="merge_ffn",
    )(x2, ya, yb, *weights)


def _mla_lane_layout():
    half = LANES // 2
    nope = np.concatenate([np.arange(HALF_ROPE, half),
                           np.arange(half + HALF_ROPE, B_NOPE + 2 * HALF_ROPE)])
    rope = np.concatenate([np.arange(HALF_ROPE), half + np.arange(HALF_ROPE)])
    return nope, rope


def kernel(x, positions, norm_mix_g, w_in, b_forget, q_a_norm_g, w_q_up, kv_a_norm_g, w_kv_up,
           w_branch_a, w_branch_b, b_gate, w_out, norm_ffn_g, w_ffn_gate, w_ffn_up, w_ffn_down,
           norm_final_g):
    batch, seq, _ = x.shape
    t = batch * seq
    assert norm_mix_g.shape[0] == 1 and seq % TM_IN == 0 and seq % TQ == 0 and t % TM_OUT == 0
    x2 = x.reshape(t, D_MODEL)
    pos2 = positions.reshape(t, 1)
    nope_lane, rope_lane = _mla_lane_layout()

    w = w_in[0].astype(BF16)
    o = np.cumsum((0, A_WIDTH, A_WIDTH, A_WIDTH, A_HEADS, B_Q_RANK, B_KV_RANK, B_ROPE,
                   D_MODEL, D_MODEL))
    wq, wk, wv = (w[:, o[i]:o[i + 1]] for i in range(3))
    wf = jnp.zeros((D_MODEL, LANES), BF16).at[:, :A_HEADS].set(w[:, o[3]:o[4]])
    wcq, wckv = w[:, o[4]:o[5]], w[:, o[5]:o[6]]
    wkr = jnp.zeros((D_MODEL, LANES), BF16).at[:, rope_lane].set(w[:, o[6]:o[7]])
    wga, wgb = w[:, o[7]:o[8]], w[:, o[8]:o[9]]

    wqu = w_q_up[0].astype(BF16).reshape(B_Q_RANK, B_HEADS, B_NOPE + B_ROPE)
    wqup = jnp.zeros((B_Q_RANK, B_HEADS, B_HEAD_PAD), BF16)
    wqup = wqup.at[:, :, nope_lane].set(wqu[:, :, :B_NOPE]).at[:, :, rope_lane].set(wqu[:, :, B_NOPE:])
    wqup = wqup.reshape(B_Q_RANK, B_HEADS * B_HEAD_PAD)
    wkvu = w_kv_up[0].astype(BF16).reshape(B_KV_RANK, B_HEADS, B_NOPE + B_V_DIM)
    wkup = jnp.zeros((B_KV_RANK, B_HEADS, B_HEAD_PAD), BF16).at[:, :, nope_lane].set(wkvu[:, :, :B_NOPE])
    wkup = wkup.reshape(B_KV_RANK, B_HEADS * B_HEAD_PAD)
    wvup = wkvu[:, :, B_NOPE:].reshape(B_KV_RANK, B_WIDTH)

    inv_freq = ROPE_BASE ** (-jnp.arange(HALF_ROPE, dtype=F32) / HALF_ROPE)
    invf = jnp.zeros((1, LANES), F32).at[0, rope_lane].set(jnp.tile(inv_freq, 2))
    sgn = jnp.zeros((1, LANES), F32).at[0, rope_lane].set(
        jnp.concatenate([-jnp.ones(HALF_ROPE, F32), jnp.ones(HALF_ROPE, F32)]))
    bf = jnp.broadcast_to(b_forget[0].astype(F32)[:, None], (A_HEADS, TM_IN))
    row = lambda v: v.astype(F32).reshape(1, -1)

    qat, ka, vat, cb, qbt, kb, vbt = _inproj(
        x2, pos2, row(norm_mix_g[0]), wq, wk, wv, wf, wcq, wckv, wkr, bf, invf, sgn,
        row(q_a_norm_g[0]), wqup, row(kv_a_norm_g[0]), wkup, wvup, batch, seq)

    ya = _attention(qat, ka, vat, cb, batch, seq, fox=True)
    yb = _attention(qbt, kb, vbt, None, batch, seq, fox=False)

    out = _post(
        x2, ya, yb, row(norm_mix_g[0]), wga, wgb, row(b_gate[0, :D_MODEL]), row(b_gate[0, D_MODEL:]),
        w_branch_a[0].astype(BF16), w_branch_b[0].astype(BF16), w_out[0].astype(BF16),
        row(norm_ffn_g[0]), w_ffn_gate[0].astype(BF16), w_ffn_up[0].astype(BF16),
        w_ffn_down[0].astype(BF16), row(norm_final_g))
    return out.reshape(batch, seq, D_MODEL)
```

```python
import functools

import jax
import jax.numpy as jnp
import numpy as np
from jax import lax
from jax.experimental import pallas as pl
from jax.experimental.pallas import tpu as pltpu

D_MODEL = 1024
CHUNK = 64
A_HEADS = 8
A_HEAD_DIM = 64
A_WIDTH = A_HEADS * A_HEAD_DIM
B_HEADS = 8
B_Q_RANK = 256
B_KV_RANK = 128
B_NOPE = 64
B_ROPE = 32
B_V_DIM = 64
B_WIDTH = B_HEADS * B_V_DIM
ROPE_BASE = 10000.0
FFN_HIDDEN = 2816
NORM_EPS = 1e-6
MASK_VALUE = -1e30

LANES = 128
BF16_ROWS = 16
LOG2E = 1.4426950408889634
A_QSCALE = (A_HEAD_DIM ** -0.5) * LOG2E
B_QSCALE = ((B_NOPE + B_ROPE) ** -0.5) * LOG2E
HALF_ROPE = B_ROPE // 2
B_HEAD_PAD = LANES
N_PAIRS = A_HEADS // 2
BIAS_PIECES = 3

TM_IN = 512
TM_SUB = 512
TM_OUT = 256
TQ = 512
TK = 512
SCORE_SLOTS = 3
QB = 256
FFN_CHUNK = 256
VMEM_LIMIT = 56 * 1024 * 1024

F32 = jnp.float32
BF16 = jnp.bfloat16


def _rms(x, g):
    ms = jnp.mean(x * x, axis=-1, keepdims=True)
    return x * lax.rsqrt(ms + NORM_EPS) * g


def _dot(a, b):
    return jnp.dot(a, b, preferred_element_type=F32)


def _rope(t, cos, sin_signed):
    return t * cos + pltpu.roll(t, LANES // 2, axis=1) * sin_signed


def _inproj_kernel(x_ref, pos_ref, g_ref, wq_ref, wk_ref, wv_ref, wf_ref, wcq_ref,
                   wckv_ref, wkr_ref, bf_ref, invf_ref, gq_ref, wqup_ref,
                   gkv_ref, wkup_ref, wvup_ref,
                   qat_ref, ka_ref, vat_ref, cb_ref, qbt_ref, kb_ref, vbt_ref, carry_ref):
    @pl.when(pl.program_id(1) == 0)
    def _():
        carry_ref[...] = jnp.zeros_like(carry_ref)

    for r0 in range(0, x_ref.shape[0], TM_SUB):
        _inproj_rows(slice(r0, r0 + TM_SUB), x_ref, pos_ref, g_ref, wq_ref, wk_ref, wv_ref,
                     wf_ref, wcq_ref, wckv_ref, wkr_ref, bf_ref, invf_ref, gq_ref, wqup_ref,
                     gkv_ref, wkup_ref, wvup_ref, qat_ref, ka_ref, vat_ref, cb_ref, qbt_ref,
                     kb_ref, vbt_ref, carry_ref)


def _inproj_rows(rows, x_ref, pos_ref, g_ref, wq_ref, wk_ref, wv_ref, wf_ref, wcq_ref,
                 wckv_ref, wkr_ref, bf_ref, invf_ref, gq_ref, wqup_ref, gkv_ref, wkup_ref,
                 wvup_ref, qat_ref, ka_ref, vat_ref, cb_ref, qbt_ref, kb_ref, vbt_ref,
                 carry_ref):
    tm = rows.stop - rows.start
    xn = _rms(x_ref[rows, :], g_ref[...]).astype(BF16)

    def store_transposed(ref, val, groups):
        for i in range(groups):
            ref[i, :, rows] = val[:, i * LANES:(i + 1) * LANES].T.astype(BF16)

    store_transposed(qat_ref, _dot(xn, wq_ref[...]) * A_QSCALE, N_PAIRS)
    ka_ref[rows, :] = _dot(xn, wk_ref[...]).astype(BF16)
    store_transposed(vat_ref, _dot(xn, wv_ref[...]), N_PAIRS)

    z = _dot(xn, wf_ref[...]).T[:A_HEADS] + bf_ref[:, rows]
    c = jnp.minimum(z, 0.0) - jnp.log1p(jnp.exp(-jnp.abs(z)))
    lane = lax.broadcasted_iota(jnp.int32, c.shape, 1)
    d = 1
    while d < tm:
        c = c + jnp.where(lane >= d, pltpu.roll(c, d, axis=1), 0.0)
        d *= 2
    c = c + jnp.tile(carry_ref[...], (1, tm // LANES))
    carry_ref[...] = jnp.broadcast_to(c[:, tm - 1:tm], carry_ref.shape)
    c = c * LOG2E
    hi = c.astype(BF16).astype(F32)
    mid = (c - hi).astype(BF16).astype(F32)
    lo = c - hi - mid
    pad = jnp.zeros((LANES - BIAS_PIECES * A_HEADS, tm), F32)
    cb_ref[rows, :] = jnp.concatenate([hi, mid, lo, pad], axis=0).T.astype(BF16)

    ang = invf_ref[...] * pos_ref[:, rows].astype(F32)
    cos_h, sin_h = jnp.cos(ang), jnp.sin(ang)
    fill = (LANES // 2 - HALF_ROPE, tm)
    cos = jnp.concatenate([cos_h, jnp.ones(fill, F32)] * 2, axis=0).T
    sin_signed = jnp.concatenate(
        [-sin_h, jnp.zeros(fill, F32), sin_h, jnp.zeros(fill, F32)], axis=0).T

    cqn = _rms(_dot(xn, wcq_ref[...]), gq_ref[...]).astype(BF16)
    qb = _dot(cqn, wqup_ref[...])
    for h in range(B_HEADS):
        sl = slice(h * B_HEAD_PAD, (h + 1) * B_HEAD_PAD)
        qbt_ref[h, :, rows] = (_rope(qb[:, sl], cos, sin_signed) * B_QSCALE).T.astype(BF16)

    ckvn = _rms(_dot(xn, wckv_ref[...]), gkv_ref[...]).astype(BF16)
    store_transposed(vbt_ref, _dot(ckvn, wvup_ref[...]), N_PAIRS)
    kn = _dot(ckvn, wkup_ref[...])
    kr = _rope(_dot(xn, wkr_ref[...]), cos, sin_signed)
    for h in range(B_HEADS):
        sl = slice(h * B_HEAD_PAD, (h + 1) * B_HEAD_PAD)
        kb_ref[rows, sl] = (kn[:, sl] + kr).astype(BF16)


def _full(shape):
    return pl.BlockSpec(shape, lambda *_: (0,) * len(shape))


def _inproj(x2, pos3, g, wq, wk, wv, wf, wcq, wckv, wkr, bf, invf, gq, wqup, gkv,
            wkup, wvup, batch, seq):
    t = x2.shape[0]
    nt = seq // TM_IN
    row = lambda w: pl.BlockSpec((TM_IN, w), lambda b, s: (b * nt + s, 0))
    col = lambda n: pl.BlockSpec((None, n, LANES, TM_IN), lambda b, s: (b, 0, 0, s))
    col_shape = lambda n: jax.ShapeDtypeStruct((batch, n, LANES, seq), BF16)
    pos_spec = pl.BlockSpec((None, 1, TM_IN), lambda b, s: (b, 0, s))
    weights = (g, wq, wk, wv, wf, wcq, wckv, wkr, bf, invf, gq, wqup, gkv, wkup, wvup)
    return pl.pallas_call(
        _inproj_kernel,
        grid=(batch, nt),
        in_specs=[row(D_MODEL), pos_spec] + [_full(w.shape) for w in weights],
        out_specs=[
            col(N_PAIRS), row(A_WIDTH), col(N_PAIRS), row(LANES),
            col(B_HEADS), row(B_HEADS * B_HEAD_PAD), col(N_PAIRS),
        ],
        out_shape=[
            col_shape(N_PAIRS),
            jax.ShapeDtypeStruct((t, A_WIDTH), BF16),
            col_shape(N_PAIRS),
            jax.ShapeDtypeStruct((t, LANES), BF16),
            col_shape(B_HEADS),
            jax.ShapeDtypeStruct((t, B_HEADS * B_HEAD_PAD), BF16),
            col_shape(N_PAIRS),
        ],
        scratch_shapes=[pltpu.VMEM((A_HEADS, LANES), F32)],
        compiler_params=pltpu.CompilerParams(
            dimension_semantics=("arbitrary", "arbitrary"), vmem_limit_bytes=VMEM_LIMIT),
        name="inproj",
    )(x2, pos3, *weights)


def _attn_kernel(*refs, fox, nq):
    if fox:
        qt_ref, k_ref, vt_ref, cb_ref, o_ref, m_scr, acc_scr, s_scr = refs
    else:
        qt_ref, k_ref, vt_ref, o_ref, m_scr, acc_scr, s_scr = refs
    qi = pl.program_id(2)
    head_dim = LANES // 2
    if fox:
        frow = lax.broadcasted_iota(jnp.int32, (LANES, TQ), 0)
        pair = pl.program_id(1)
        mine = [(frow < head_dim) == (h == 0) for h in range(2)]
        picks = [jnp.where((frow % A_HEADS == 2 * pair + h) & (frow < BIAS_PIECES * A_HEADS),
                           -1.0, 0.0).astype(BF16) for h in range(2)]

    def query_operands(tile):
        q0 = pl.multiple_of(tile * TQ, TQ)
        if fox:
            qt = qt_ref[:, pl.ds(q0, TQ)]
            return [jnp.concatenate([jnp.where(mine[h], qt, jnp.zeros_like(qt)), picks[h]], axis=0)
                    for h in range(2)]
        return [qt_ref[h, :, pl.ds(q0, TQ)] for h in range(2)]

    rhs = query_operands(qi)
    rhs_next = query_operands(jnp.minimum(qi + 1, nq - 1))

    m_scr[...] = jnp.full_like(m_scr, MASK_VALUE)
    acc_scr[...] = jnp.zeros_like(acc_scr)

    keys = lax.broadcasted_iota(jnp.int32, (TK, TQ), 0)
    queries = lax.broadcasted_iota(jnp.int32, (TK, TQ), 1)
    if fox:
        diag_mask = keys <= queries
    else:
        diag_mask = (keys // CHUNK) <= (queries // CHUNK)
    ones = jnp.ones((BF16_ROWS, TK), BF16)

    units = [(h, slice(c, c + QB)) for h in range(2) for c in range(0, TQ, QB)]

    def scores(j, h, cols, operands):
        start = pl.multiple_of(j * TK, TK)
        if fox:
            lhs = jnp.concatenate([k_ref[pl.ds(start, TK), :], cb_ref[pl.ds(start, TK), :]], axis=1)
        else:
            lhs = k_ref[pl.ds(start, TK), h * LANES:(h + 1) * LANES]
        return _dot(lhs, operands[h][:, cols])

    def update(j, slot, h, cols, masked):
        start = pl.multiple_of(j * TK, TK)
        s = s_scr[slot, h, :, cols]
        if masked:
            s = jnp.where(diag_mask[:, cols], s, MASK_VALUE)
        m_prev = m_scr[h, :, cols]
        m_new = jnp.maximum(m_prev, jnp.max(s, axis=0, keepdims=True))
        alpha = jnp.exp2(m_prev - m_new)
        p = jnp.exp2(s - m_new).astype(BF16)
        vt = jnp.concatenate(
            [vt_ref[h * head_dim:(h + 1) * head_dim, pl.ds(start, TK)], ones], axis=0)
        acc_scr[h, :, cols] = alpha * acc_scr[h, :, cols] + _dot(vt, p)
        m_scr[h, :, cols] = m_new

    first_slot = SCORE_SLOTS

    def stage(j, slot):
        nxt = 1 if slot == first_slot else (slot + 1) % SCORE_SLOTS
        for i in range(len(units) + 1):
            if i < len(units):
                h, cols = units[i]
                s_scr[nxt, h, :, cols] = scores(j + 1, h, cols, rhs)
            if i > 0:
                update(j, slot, *units[i - 1], False)

    def last_stage(j, slot):
        for h, cols in units:
            s_scr[first_slot, h, :, cols] = scores(0, h, cols, rhs_next)
            update(j, slot, h, cols, True)

    @pl.when(qi == 0)
    def _():
        for h, cols in units:
            s_scr[0, h, :, cols] = scores(0, h, cols, rhs)
        last_stage(0, 0)

    @pl.when(qi > 0)
    def _():
        stage(0, first_slot)

        def body(i, carry):
            for u in range(SCORE_SLOTS):
                stage(SCORE_SLOTS * i + 1 + u, (1 + u) % SCORE_SLOTS)
            return carry

        n_rounds = (qi - 1) // SCORE_SLOTS
        lax.fori_loop(0, n_rounds, body, 0)
        base = n_rounds * SCORE_SLOTS
        for r in range(SCORE_SLOTS):
            @pl.when(qi - 1 - base == r)
            def _(r=r):
                for u in range(r):
                    stage(base + 1 + u, (1 + u) % SCORE_SLOTS)
                last_stage(qi, (1 + r) % SCORE_SLOTS)

    out_t = jnp.concatenate(
        [acc_scr[h, :head_dim] / acc_scr[h, head_dim:head_dim + 1] for h in range(2)], axis=0)
    o_ref[...] = out_t.T.astype(o_ref.dtype)


def _attention(qt, k, vt, cb, batch, seq, fox):
    t = k.shape[0]
    nq = seq // TQ
    assert TQ == TK
    qt_spec = pl.BlockSpec((None, None if fox else 2, LANES, seq), lambda b, p, i: (b, p, 0, 0))
    in_specs = [
        qt_spec,
        pl.BlockSpec((seq, LANES if fox else 2 * LANES), lambda b, p, i: (b, p)),
        pl.BlockSpec((None, None, LANES, seq), lambda b, p, i: (b, p, 0, 0)),
    ]
    args = [qt, k, vt]
    if fox:
        in_specs.append(pl.BlockSpec((seq, LANES), lambda b, p, i: (b, 0)))
        args.append(cb)
    acc_rows = LANES // 2 + BF16_ROWS
    return pl.pallas_call(
        functools.partial(_attn_kernel, fox=fox, nq=nq),
        grid=(batch, N_PAIRS, nq),
        in_specs=in_specs,
        out_specs=pl.BlockSpec((TQ, LANES), lambda b, p, i: (b * nq + i, p)),
        out_shape=jax.ShapeDtypeStruct((t, A_WIDTH), BF16),
        scratch_shapes=[pltpu.VMEM((2, 1, TQ), F32), pltpu.VMEM((2, acc_rows, TQ), F32),
                        pltpu.VMEM((SCORE_SLOTS + 1, 2, TK, TQ), F32)],
        compiler_params=pltpu.CompilerParams(
            dimension_semantics=("arbitrary", "arbitrary", "arbitrary"),
            vmem_limit_bytes=VMEM_LIMIT),
        name="fox_attention" if fox else "mla_attention",
    )(*args)


def _post_kernel(x_ref, ya_ref, yb_ref, g_ref, wga_ref, wgb_ref, bga_ref, bgb_ref, wa_ref,
                 wb_ref, wout_ref, gffn_ref, wgate_ref, wup_ref, wdown_ref, gfin_ref,
                 o_ref, act_scr):
    x = x_ref[...]
    xn = _rms(x, g_ref[...]).astype(BF16)
    gate_a = jax.nn.sigmoid(_dot(xn, wga_ref[...]) + bga_ref[...])
    gate_b = jax.nn.sigmoid(_dot(xn, wgb_ref[...]) + bgb_ref[...])
    merged = gate_a * _dot(ya_ref[...], wa_ref[...]) + gate_b * _dot(yb_ref[...], wb_ref[...])
    h = x + _dot(merged.astype(BF16), wout_ref[...])
    hn = _rms(h, gffn_ref[...]).astype(BF16)
    for c in range(0, FFN_HIDDEN, FFN_CHUNK):
        sl = slice(c, c + FFN_CHUNK)
        gt = _dot(hn, wgate_ref[:, sl])
        act_scr[:, sl] = (gt * jax.nn.sigmoid(gt) * _dot(hn, wup_ref[:, sl])).astype(BF16)
    h = h + _dot(act_scr[...], wdown_ref[...])
    o_ref[...] = _rms(h, gfin_ref[...])


def _post(x2, ya, yb, *weights):
    t = x2.shape[0]
    row = lambda w: pl.BlockSpec((TM_OUT, w), lambda i: (i, 0))
    const = lambda w: pl.BlockSpec(w.shape, lambda i: (0,) * w.ndim,
                                   pipeline_mode=pl.Buffered(1))
    return pl.pallas_call(
        _post_kernel,
        grid=(t // TM_OUT,),
        in_specs=[row(D_MODEL), row(A_WIDTH), row(B_WIDTH)] + [const(w) for w in weights],
        out_specs=row(D_MODEL),
        out_shape=jax.ShapeDtypeStruct((t, D_MODEL), F32),
        scratch_shapes=[pltpu.VMEM((TM_OUT, FFN_HIDDEN), BF16)],
        compiler_params=pltpu.CompilerParams(
            dimension_semantics=("arbitrary",), vmem_limit_bytes=VMEM_LIMIT),
        name="merge_ffn",
    )(x2, ya, yb, *weights)


def _mla_lane_layout():
    half = LANES // 2
    nope = np.concatenate([np.arange(HALF_ROPE, half),
                           np.arange(half + HALF_ROPE, B_NOPE + 2 * HALF_ROPE)])
    rope = np.concatenate([np.arange(HALF_ROPE), half + np.arange(HALF_ROPE)])
    return nope, rope


def kernel(x, positions, norm_mix_g, w_in, b_forget, q_a_norm_g, w_q_up, kv_a_norm_g, w_kv_up,
           w_branch_a, w_branch_b, b_gate, w_out, norm_ffn_g, w_ffn_gate, w_ffn_up, w_ffn_down,
           norm_final_g):
    batch, seq, _ = x.shape
    t = batch * seq
    assert norm_mix_g.shape[0] == 1 and seq % TM_IN == 0 and seq % TQ == 0 and t % TM_OUT == 0
    x2 = x.reshape(t, D_MODEL)
    pos3 = positions.reshape(batch, 1, seq)
    nope_lane, rope_lane = _mla_lane_layout()

    w = w_in[0].astype(BF16)
    o = np.cumsum((0, A_WIDTH, A_WIDTH, A_WIDTH, A_HEADS, B_Q_RANK, B_KV_RANK, B_ROPE,
                   D_MODEL, D_MODEL))
    wq, wk, wv = (w[:, o[i]:o[i + 1]] for i in range(3))
    wf = jnp.zeros((D_MODEL, LANES), BF16).at[:, :A_HEADS].set(w[:, o[3]:o[4]])
    wcq, wckv = w[:, o[4]:o[5]], w[:, o[5]:o[6]]
    wkr = jnp.zeros((D_MODEL, LANES), BF16).at[:, rope_lane].set(w[:, o[6]:o[7]])
    wga, wgb = w[:, o[7]:o[8]], w[:, o[8]:o[9]]

    wqu = w_q_up[0].astype(BF16).reshape(B_Q_RANK, B_HEADS, B_NOPE + B_ROPE)
    wqup = jnp.zeros((B_Q_RANK, B_HEADS, B_HEAD_PAD), BF16)
    wqup = wqup.at[:, :, nope_lane].set(wqu[:, :, :B_NOPE]).at[:, :, rope_lane].set(wqu[:, :, B_NOPE:])
    wqup = wqup.reshape(B_Q_RANK, B_HEADS * B_HEAD_PAD)
    wkvu = w_kv_up[0].astype(BF16).reshape(B_KV_RANK, B_HEADS, B_NOPE + B_V_DIM)
    wkup = jnp.zeros((B_KV_RANK, B_HEADS, B_HEAD_PAD), BF16).at[:, :, nope_lane].set(wkvu[:, :, :B_NOPE])
    wkup = wkup.reshape(B_KV_RANK, B_HEADS * B_HEAD_PAD)
    wvup = wkvu[:, :, B_NOPE:].reshape(B_KV_RANK, B_WIDTH)

    invf = (ROPE_BASE ** (-jnp.arange(HALF_ROPE, dtype=F32) / HALF_ROPE)).reshape(HALF_ROPE, 1)
    bf = jnp.broadcast_to(b_forget[0].astype(F32)[:, None], (A_HEADS, TM_IN))
    row = lambda v: v.astype(F32).reshape(1, -1)

    qat, ka, vat, cb, qbt, kb, vbt = _inproj(
        x2, pos3, row(norm_mix_g[0]), wq, wk, wv, wf, wcq, wckv, wkr, bf, invf,
        row(q_a_norm_g[0]), wqup, row(kv_a_norm_g[0]), wkup, wvup, batch, seq)

    ya = _attention(qat, ka, vat, cb, batch, seq, fox=True)
    yb = _attention(qbt, kb, vbt, None, batch, seq, fox=False)

    out = _post(
        x2, ya, yb, row(norm_mix_g[0]), wga, wgb, row(b_gate[0, :D_MODEL]), row(b_gate[0, D_MODEL:]),
        w_branch_a[0].astype(BF16), w_branch_b[0].astype(BF16), w_out[0].astype(BF16),
        row(norm_ffn_g[0]), w_ffn_gate[0].astype(BF16), w_ffn_up[0].astype(BF16),
        w_ffn_down[0].astype(BF16), row(norm_final_g))
    return out.reshape(batch, seq, D_MODEL)
```

```python
import functools

import jax
import jax.numpy as jnp
import numpy as np
from jax import lax
from jax.experimental import pallas as pl
from jax.experimental.pallas import tpu as pltpu

D_MODEL = 1024
CHUNK = 64
A_HEADS = 8
A_HEAD_DIM = 64
A_WIDTH = A_HEADS * A_HEAD_DIM
B_HEADS = 8
B_Q_RANK = 256
B_KV_RANK = 128
B_NOPE = 64
B_ROPE = 32
B_V_DIM = 64
B_WIDTH = B_HEADS * B_V_DIM
ROPE_BASE = 10000.0
FFN_HIDDEN = 2816
NORM_EPS = 1e-6
MASK_VALUE = -1e30

LANES = 128
BF16_ROWS = 16
LOG2E = 1.4426950408889634
A_QSCALE = (A_HEAD_DIM ** -0.5) * LOG2E
B_QSCALE = ((B_NOPE + B_ROPE) ** -0.5) * LOG2E
HALF_ROPE = B_ROPE // 2
B_HEAD_PAD = LANES
N_PAIRS = A_HEADS // 2
BIAS_PIECES = 3

TM_IN = 512
TM_SUB = 512
TM_OUT = 256
TQ = 512
TK = 512
SCORE_SLOTS = 3
QB = 256
FFN_CHUNK = 256
VMEM_LIMIT = 56 * 1024 * 1024

F32 = jnp.float32
BF16 = jnp.bfloat16


def _rms(x, g):
    ms = jnp.mean(x * x, axis=-1, keepdims=True)
    return x * lax.rsqrt(ms + NORM_EPS) * g


def _dot(a, b):
    return jnp.dot(a, b, preferred_element_type=F32)


def _rope(t, cos, sin_signed):
    return t * cos + pltpu.roll(t, LANES // 2, axis=1) * sin_signed


def _inproj_kernel(x_ref, pos_ref, g_ref, wq_ref, wk_ref, wv_ref, wf_ref, wcq_ref,
                   wckv_ref, wkr_ref, bf_ref, invf_ref, gq_ref, wqup_ref,
                   gkv_ref, wkup_ref, wvup_ref,
                   qat_ref, ka_ref, vat_ref, cb_ref, qbt_ref, kb_ref, vbt_ref, carry_ref):
    @pl.when(pl.program_id(1) == 0)
    def _():
        carry_ref[...] = jnp.zeros_like(carry_ref)

    for r0 in range(0, x_ref.shape[0], TM_SUB):
        _inproj_rows(slice(r0, r0 + TM_SUB), x_ref, pos_ref, g_ref, wq_ref, wk_ref, wv_ref,
                     wf_ref, wcq_ref, wckv_ref, wkr_ref, bf_ref, invf_ref, gq_ref, wqup_ref,
                     gkv_ref, wkup_ref, wvup_ref, qat_ref, ka_ref, vat_ref, cb_ref, qbt_ref,
                     kb_ref, vbt_ref, carry_ref)


def _inproj_rows(rows, x_ref, pos_ref, g_ref, wq_ref, wk_ref, wv_ref, wf_ref, wcq_ref,
                 wckv_ref, wkr_ref, bf_ref, invf_ref, gq_ref, wqup_ref, gkv_ref, wkup_ref,
                 wvup_ref, qat_ref, ka_ref, vat_ref, cb_ref, qbt_ref, kb_ref, vbt_ref,
                 carry_ref):
    tm = rows.stop - rows.start
    xn = _rms(x_ref[rows, :], g_ref[...]).astype(BF16)

    def store_transposed(ref, val, groups):
        for i in range(groups):
            ref[i, :, rows] = val[:, i * LANES:(i + 1) * LANES].T.astype(BF16)

    store_transposed(qat_ref, _dot(xn, wq_ref[...]) * A_QSCALE, N_PAIRS)
    ka_ref[rows, :] = _dot(xn, wk_ref[...]).astype(BF16)
    store_transposed(vat_ref, _dot(xn, wv_ref[...]), N_PAIRS)

    z = _dot(xn, wf_ref[...]).T[:A_HEADS] + bf_ref[:, rows]
    c = jnp.minimum(z, 0.0) - jnp.log1p(jnp.exp(-jnp.abs(z)))
    lane = lax.broadcasted_iota(jnp.int32, c.shape, 1)
    d = 1
    while d < tm:
        c = c + jnp.where(lane >= d, pltpu.roll(c, d, axis=1), 0.0)
        d *= 2
    c = c + jnp.tile(carry_ref[...], (1, tm // LANES))
    carry_ref[...] = jnp.broadcast_to(c[:, tm - 1:tm], carry_ref.shape)
    c = c * LOG2E
    hi = c.astype(BF16).astype(F32)
    mid = (c - hi).astype(BF16).astype(F32)
    lo = c - hi - mid
    pad = jnp.zeros((LANES - BIAS_PIECES * A_HEADS, tm), F32)
    cb_ref[rows, :] = jnp.concatenate([hi, mid, lo, pad], axis=0).T.astype(BF16)

    ang = invf_ref[...] * pos_ref[:, rows].astype(F32)
    cos_h, sin_h = jnp.cos(ang), jnp.sin(ang)
    fill = (LANES // 2 - HALF_ROPE, tm)
    cos = jnp.concatenate([cos_h, jnp.ones(fill, F32)] * 2, axis=0).T
    sin_signed = jnp.concatenate(
        [-sin_h, jnp.zeros(fill, F32), sin_h, jnp.zeros(fill, F32)], axis=0).T

    cqn = _rms(_dot(xn, wcq_ref[...]), gq_ref[...]).astype(BF16)
    qb = _dot(cqn, wqup_ref[...])
    for h in range(B_HEADS):
        sl = slice(h * B_HEAD_PAD, (h + 1) * B_HEAD_PAD)
        qbt_ref[h, :, rows] = (_rope(qb[:, sl], cos, sin_signed) * B_QSCALE).T.astype(BF16)

    ckvn = _rms(_dot(xn, wckv_ref[...]), gkv_ref[...]).astype(BF16)
    store_transposed(vbt_ref, _dot(ckvn, wvup_ref[...]), N_PAIRS)
    kn = _dot(ckvn, wkup_ref[...])
    kr = _rope(_dot(xn, wkr_ref[...]), cos, sin_signed)
    for h in range(B_HEADS):
        sl = slice(h * B_HEAD_PAD, (h + 1) * B_HEAD_PAD)
        kb_ref[rows, sl] = (kn[:, sl] + kr).astype(BF16)


def _full(shape):
    return pl.BlockSpec(shape, lambda *_: (0,) * len(shape))


def _inproj(x2, pos3, g, wq, wk, wv, wf, wcq, wckv, wkr, bf, invf, gq, wqup, gkv,
            wkup, wvup, batch, seq):
    t = x2.shape[0]
    nt = seq // TM_IN
    row = lambda w: pl.BlockSpec((TM_IN, w), lambda b, s: (b * nt + s, 0))
    col = lambda n: pl.BlockSpec((None, n, LANES, TM_IN), lambda b, s: (b, 0, 0, s))
    col_shape = lambda n: jax.ShapeDtypeStruct((batch, n, LANES, seq), BF16)
    pos_spec = pl.BlockSpec((None, 1, TM_IN), lambda b, s: (b, 0, s))
    weights = (g, wq, wk, wv, wf, wcq, wckv, wkr, bf, invf, gq, wqup, gkv, wkup, wvup)
    return pl.pallas_call(
        _inproj_kernel,
        grid=(batch, nt),
        in_specs=[row(D_MODEL), pos_spec] + [_full(w.shape) for w in weights],
        out_specs=[
            col(N_PAIRS), row(A_WIDTH), col(N_PAIRS), row(LANES),
            col(B_HEADS), row(B_HEADS * B_HEAD_PAD), col(N_PAIRS),
        ],
        out_shape=[
            col_shape(N_PAIRS),
            jax.ShapeDtypeStruct((t, A_WIDTH), BF16),
            col_shape(N_PAIRS),
            jax.ShapeDtypeStruct((t, LANES), BF16),
            col_shape(B_HEADS),
            jax.ShapeDtypeStruct((t, B_HEADS * B_HEAD_PAD), BF16),
            col_shape(N_PAIRS),
        ],
        scratch_shapes=[pltpu.VMEM((A_HEADS, LANES), F32)],
        compiler_params=pltpu.CompilerParams(
            dimension_semantics=("arbitrary", "arbitrary"), vmem_limit_bytes=VMEM_LIMIT),
        name="inproj",
    )(x2, pos3, *weights)


def _attn_kernel(*refs, fox, nq):
    def tile(qi, carry):
        _attn_tile(qi, refs, fox, nq)
        return carry

    lax.fori_loop(0, nq, tile, 0)


def _attn_tile(qi, refs, fox, nq):
    if fox:
        qt_ref, k_ref, vt_ref, cb_ref, o_ref, m_scr, acc_scr, s_scr = refs
    else:
        qt_ref, k_ref, vt_ref, o_ref, m_scr, acc_scr, s_scr = refs
    head_dim = LANES // 2
    if fox:
        frow = lax.broadcasted_iota(jnp.int32, (LANES, TQ), 0)
        pair = pl.program_id(1)
        mine = [(frow < head_dim) == (h == 0) for h in range(2)]
        picks = [jnp.where((frow % A_HEADS == 2 * pair + h) & (frow < BIAS_PIECES * A_HEADS),
                           -1.0, 0.0).astype(BF16) for h in range(2)]

    def query_operands(tile):
        q0 = pl.multiple_of(tile * TQ, TQ)
        if fox:
            qt = qt_ref[:, pl.ds(q0, TQ)]
            return [jnp.concatenate([jnp.where(mine[h], qt, jnp.zeros_like(qt)), picks[h]], axis=0)
                    for h in range(2)]
        return [qt_ref[h, :, pl.ds(q0, TQ)] for h in range(2)]

    rhs = query_operands(qi)
    rhs_next = query_operands(jnp.minimum(qi + 1, nq - 1))

    m_scr[...] = jnp.full_like(m_scr, MASK_VALUE)
    acc_scr[...] = jnp.zeros_like(acc_scr)

    keys = lax.broadcasted_iota(jnp.int32, (TK, TQ), 0)
    queries = lax.broadcasted_iota(jnp.int32, (TK, TQ), 1)
    if fox:
        diag_mask = keys <= queries
    else:
        diag_mask = (keys // CHUNK) <= (queries // CHUNK)
    ones = jnp.ones((BF16_ROWS, TK), BF16)

    units = [(h, slice(c, c + QB)) for h in range(2) for c in range(0, TQ, QB)]

    def scores(j, h, cols, operands):
        start = pl.multiple_of(j * TK, TK)
        if fox:
            lhs = jnp.concatenate([k_ref[pl.ds(start, TK), :], cb_ref[pl.ds(start, TK), :]], axis=1)
        else:
            lhs = k_ref[pl.ds(start, TK), h * LANES:(h + 1) * LANES]
        return _dot(lhs, operands[h][:, cols])

    def update(j, slot, h, cols, masked):
        start = pl.multiple_of(j * TK, TK)
        s = s_scr[slot, h, :, cols]
        if masked:
            s = jnp.where(diag_mask[:, cols], s, MASK_VALUE)
        m_prev = m_scr[h, :, cols]
        m_new = jnp.maximum(m_prev, jnp.max(s, axis=0, keepdims=True))
        alpha = jnp.exp2(m_prev - m_new)
        p = jnp.exp2(s - m_new).astype(BF16)
        vt = jnp.concatenate(
            [vt_ref[h * head_dim:(h + 1) * head_dim, pl.ds(start, TK)], ones], axis=0)
        acc_scr[h, :, cols] = alpha * acc_scr[h, :, cols] + _dot(vt, p)
        m_scr[h, :, cols] = m_new

    first_slot = SCORE_SLOTS

    def stage(j, slot):
        nxt = 1 if slot == first_slot else (slot + 1) % SCORE_SLOTS
        for i in range(len(units) + 1):
            if i < len(units):
                h, cols = units[i]
                s_scr[nxt, h, :, cols] = scores(j + 1, h, cols, rhs)
            if i > 0:
                update(j, slot, *units[i - 1], False)

    def last_stage(j, slot):
        for h, cols in units:
            s_scr[first_slot, h, :, cols] = scores(0, h, cols, rhs_next)
            update(j, slot, h, cols, True)

    @pl.when(qi == 0)
    def _():
        for h, cols in units:
            s_scr[0, h, :, cols] = scores(0, h, cols, rhs)
        last_stage(0, 0)

    @pl.when(qi > 0)
    def _():
        stage(0, first_slot)

        def body(i, carry):
            for u in range(SCORE_SLOTS):
                stage(SCORE_SLOTS * i + 1 + u, (1 + u) % SCORE_SLOTS)
            return carry

        n_rounds = (qi - 1) // SCORE_SLOTS
        lax.fori_loop(0, n_rounds, body, 0)
        base = n_rounds * SCORE_SLOTS
        for r in range(SCORE_SLOTS):
            @pl.when(qi - 1 - base == r)
            def _(r=r):
                for u in range(r):
                    stage(base + 1 + u, (1 + u) % SCORE_SLOTS)
                last_stage(qi, (1 + r) % SCORE_SLOTS)

    out_t = jnp.concatenate(
        [acc_scr[h, :head_dim] / acc_scr[h, head_dim:head_dim + 1] for h in range(2)], axis=0)
    o_ref[pl.ds(pl.multiple_of(qi * TQ, TQ), TQ), :] = out_t.T.astype(o_ref.dtype)


def _attention(qt, k, vt, cb, batch, seq, fox):
    t = k.shape[0]
    nq = seq // TQ
    assert TQ == TK
    qt_spec = pl.BlockSpec((None, None if fox else 2, LANES, seq), lambda b, p: (b, p, 0, 0))
    in_specs = [
        qt_spec,
        pl.BlockSpec((seq, LANES if fox else 2 * LANES), lambda b, p: (b, p)),
        pl.BlockSpec((None, None, LANES, seq), lambda b, p: (b, p, 0, 0)),
    ]
    args = [qt, k, vt]
    if fox:
        in_specs.append(pl.BlockSpec((seq, LANES), lambda b, p: (b, 0)))
        args.append(cb)
    acc_rows = LANES // 2 + BF16_ROWS
    return pl.pallas_call(
        functools.partial(_attn_kernel, fox=fox, nq=nq),
        grid=(batch, N_PAIRS),
        in_specs=in_specs,
        out_specs=pl.BlockSpec((seq, LANES), lambda b, p: (b, p)),
        out_shape=jax.ShapeDtypeStruct((t, A_WIDTH), BF16),
        scratch_shapes=[pltpu.VMEM((2, 1, TQ), F32), pltpu.VMEM((2, acc_rows, TQ), F32),
                        pltpu.VMEM((SCORE_SLOTS + 1, 2, TK, TQ), F32)],
        compiler_params=pltpu.CompilerParams(
            dimension_semantics=("arbitrary", "arbitrary"),
            vmem_limit_bytes=VMEM_LIMIT),
        name="fox_attention" if fox else "mla_attention",
    )(*args)


def _post_kernel(x_ref, ya_ref, yb_ref, g_ref, wga_ref, wgb_ref, bga_ref, bgb_ref, wa_ref,
                 wb_ref, wout_ref, gffn_ref, wgate_ref, wup_ref, wdown_ref, gfin_ref,
                 o_ref, act_scr):
    x = x_ref[...]
    xn = _rms(x, g_ref[...]).astype(BF16)
    gate_a = jax.nn.sigmoid(_dot(xn, wga_ref[...]) + bga_ref[...])
    gate_b = jax.nn.sigmoid(_dot(xn, wgb_ref[...]) + bgb_ref[...])
    merged = gate_a * _dot(ya_ref[...], wa_ref[...]) + gate_b * _dot(yb_ref[...], wb_ref[...])
    h = x + _dot(merged.astype(BF16), wout_ref[...])
    hn = _rms(h, gffn_ref[...]).astype(BF16)
    for c in range(0, FFN_HIDDEN, FFN_CHUNK):
        sl = slice(c, c + FFN_CHUNK)
        gt = _dot(hn, wgate_ref[:, sl])
        act_scr[:, sl] = (gt * jax.nn.sigmoid(gt) * _dot(hn, wup_ref[:, sl])).astype(BF16)
    h = h + _dot(act_scr[...], wdown_ref[...])
    o_ref[...] = _rms(h, gfin_ref[...])


def _post(x2, ya, yb, *weights):
    t = x2.shape[0]
    row = lambda w: pl.BlockSpec((TM_OUT, w), lambda i: (i, 0))
    const = lambda w: pl.BlockSpec(w.shape, lambda i: (0,) * w.ndim,
                                   pipeline_mode=pl.Buffered(1))
    return pl.pallas_call(
        _post_kernel,
        grid=(t // TM_OUT,),
        in_specs=[row(D_MODEL), row(A_WIDTH), row(B_WIDTH)] + [const(w) for w in weights],
        out_specs=row(D_MODEL),
        out_shape=jax.ShapeDtypeStruct((t, D_MODEL), F32),
        scratch_shapes=[pltpu.VMEM((TM_OUT, FFN_HIDDEN), BF16)],
        compiler_params=pltpu.CompilerParams(
            dimension_semantics=("arbitrary",), vmem_limit_bytes=VMEM_LIMIT),
        name="merge_ffn",
    )(x2, ya, yb, *weights)


def _mla_lane_layout():
    half = LANES // 2
    nope = np.concatenate([np.arange(HALF_ROPE, half),
                           np.arange(half + HALF_ROPE, B_NOPE + 2 * HALF_ROPE)])
    rope = np.concatenate([np.arange(HALF_ROPE), half + np.arange(HALF_ROPE)])
    return nope, rope


def kernel(x, positions, norm_mix_g, w_in, b_forget, q_a_norm_g, w_q_up, kv_a_norm_g, w_kv_up,
           w_branch_a, w_branch_b, b_gate, w_out, norm_ffn_g, w_ffn_gate, w_ffn_up, w_ffn_down,
           norm_final_g):
    batch, seq, _ = x.shape
    t = batch * seq
    assert norm_mix_g.shape[0] == 1 and seq % TM_IN == 0 and seq % TQ == 0 and t % TM_OUT == 0
    x2 = x.reshape(t, D_MODEL)
    pos3 = positions.reshape(batch, 1, seq)
    nope_lane, rope_lane = _mla_lane_layout()

    w = w_in[0].astype(BF16)
    o = np.cumsum((0, A_WIDTH, A_WIDTH, A_WIDTH, A_HEADS, B_Q_RANK, B_KV_RANK, B_ROPE,
                   D_MODEL, D_MODEL))
    wq, wk, wv = (w[:, o[i]:o[i + 1]] for i in range(3))
    wf = jnp.zeros((D_MODEL, LANES), BF16).at[:, :A_HEADS].set(w[:, o[3]:o[4]])
    wcq, wckv = w[:, o[4]:o[5]], w[:, o[5]:o[6]]
    wkr = jnp.zeros((D_MODEL, LANES), BF16).at[:, rope_lane].set(w[:, o[6]:o[7]])
    wga, wgb = w[:, o[7]:o[8]], w[:, o[8]:o[9]]

    wqu = w_q_up[0].astype(BF16).reshape(B_Q_RANK, B_HEADS, B_NOPE + B_ROPE)
    wqup = jnp.zeros((B_Q_RANK, B_HEADS, B_HEAD_PAD), BF16)
    wqup = wqup.at[:, :, nope_lane].set(wqu[:, :, :B_NOPE]).at[:, :, rope_lane].set(wqu[:, :, B_NOPE:])
    wqup = wqup.reshape(B_Q_RANK, B_HEADS * B_HEAD_PAD)
    wkvu = w_kv_up[0].astype(BF16).reshape(B_KV_RANK, B_HEADS, B_NOPE + B_V_DIM)
    wkup = jnp.zeros((B_KV_RANK, B_HEADS, B_HEAD_PAD), BF16).at[:, :, nope_lane].set(wkvu[:, :, :B_NOPE])
    wkup = wkup.reshape(B_KV_RANK, B_HEADS * B_HEAD_PAD)
    wvup = wkvu[:, :, B_NOPE:].reshape(B_KV_RANK, B_WIDTH)

    invf = (ROPE_BASE ** (-jnp.arange(HALF_ROPE, dtype=F32) / HALF_ROPE)).reshape(HALF_ROPE, 1)
    bf = jnp.broadcast_to(b_forget[0].astype(F32)[:, None], (A_HEADS, TM_IN))
    row = lambda v: v.astype(F32).reshape(1, -1)

    qat, ka, vat, cb, qbt, kb, vbt = _inproj(
        x2, pos3, row(norm_mix_g[0]), wq, wk, wv, wf, wcq, wckv, wkr, bf, invf,
        row(q_a_norm_g[0]), wqup, row(kv_a_norm_g[0]), wkup, wvup, batch, seq)

    ya = _attention(qat, ka, vat, cb, batch, seq, fox=True)
    yb = _attention(qbt, kb, vbt, None, batch, seq, fox=False)

    out = _post(
        x2, ya, yb, row(norm_mix_g[0]), wga, wgb, row(b_gate[0, :D_MODEL]), row(b_gate[0, D_MODEL:]),
        w_branch_a[0].astype(BF16), w_branch_b[0].astype(BF16), w_out[0].astype(BF16),
        row(norm_ffn_g[0]), w_ffn_gate[0].astype(BF16), w_ffn_up[0].astype(BF16),
        w_ffn_down[0].astype(BF16), row(norm_final_g))
    return out.reshape(batch, seq, D_MODEL)
```

```python
import functools

import jax
import jax.numpy as jnp
import numpy as np
from jax import lax
from jax.experimental import pallas as pl
from jax.experimental.pallas import tpu as pltpu

D_MODEL = 1024
CHUNK = 64
A_HEADS = 8
A_HEAD_DIM = 64
A_WIDTH = A_HEADS * A_HEAD_DIM
B_HEADS = 8
B_Q_RANK = 256
B_KV_RANK = 128
B_NOPE = 64
B_ROPE = 32
B_V_DIM = 64
B_WIDTH = B_HEADS * B_V_DIM
ROPE_BASE = 10000.0
FFN_HIDDEN = 2816
NORM_EPS = 1e-6
MASK_VALUE = -1e30

LANES = 128
BF16_ROWS = 16
LOG2E = 1.4426950408889634
A_QSCALE = (A_HEAD_DIM ** -0.5) * LOG2E
B_QSCALE = ((B_NOPE + B_ROPE) ** -0.5) * LOG2E
HALF_ROPE = B_ROPE // 2
B_HEAD_PAD = LANES
N_PAIRS = A_HEADS // 2
BIAS_PIECES = 3

TM_IN = 512
TM_OUT = 512
TQ = 512
TK = 512
SCORE_SLOTS = 3
QB = 256
FFN_CHUNK = 256
VMEM_LIMIT = 56 * 1024 * 1024

F32 = jnp.float32
BF16 = jnp.bfloat16


def _rms(x, g):
    ms = jnp.mean(x * x, axis=-1, keepdims=True)
    return x * lax.rsqrt(ms + NORM_EPS) * g


def _dot(a, b):
    return jnp.dot(a, b, preferred_element_type=F32)


def _rope(t, cos, sin_signed):
    return t * cos + pltpu.roll(t, LANES // 2, axis=1) * sin_signed


def _inproj_kernel(x_ref, pos_ref, g_ref, wq_ref, wk_ref, wv_ref, wf_ref, wcq_ref,
                   wckv_ref, wkr_ref, bf_ref, invf_ref, gq_ref, wqup_ref,
                   gkv_ref, wkup_ref, wvup_ref,
                   qat_ref, ka_ref, vat_ref, cb_ref, qbt_ref, kb_ref, vbt_ref, carry_ref):
    @pl.when(pl.program_id(1) == 0)
    def _():
        carry_ref[...] = jnp.zeros_like(carry_ref)

    tm = x_ref.shape[0]
    xn = _rms(x_ref[...], g_ref[...]).astype(BF16)

    def store_transposed(ref, val, groups):
        for i in range(groups):
            ref[i] = val[:, i * LANES:(i + 1) * LANES].T.astype(BF16)

    f = _dot(xn, wf_ref[...])
    cq = _dot(xn, wcq_ref[...])
    ckv = _dot(xn, wckv_ref[...])
    kr = _dot(xn, wkr_ref[...])
    cqn = _rms(cq, gq_ref[...]).astype(BF16)
    ckvn = _rms(ckv, gkv_ref[...]).astype(BF16)
    qb = _dot(cqn, wqup_ref[...])
    q = _dot(xn, wq_ref[...])

    z = f.T[:A_HEADS] + bf_ref[...]
    c = jnp.minimum(z, 0.0) - jnp.log1p(jnp.exp(-jnp.abs(z)))
    lane = lax.broadcasted_iota(jnp.int32, c.shape, 1)
    d = 1
    while d < tm:
        c = c + jnp.where(lane >= d, pltpu.roll(c, d, axis=1), 0.0)
        d *= 2
    c = c + jnp.tile(carry_ref[...], (1, tm // LANES))
    carry_ref[...] = jnp.broadcast_to(c[:, tm - 1:tm], carry_ref.shape)
    c = c * LOG2E
    hi = c.astype(BF16).astype(F32)
    mid = (c - hi).astype(BF16).astype(F32)
    lo = c - hi - mid
    pad = jnp.zeros((LANES - BIAS_PIECES * A_HEADS, tm), F32)
    cb_ref[...] = jnp.concatenate([hi, mid, lo, pad], axis=0).T.astype(BF16)

    ang = invf_ref[...] * pos_ref[...].astype(F32)
    cos_h, sin_h = jnp.cos(ang), jnp.sin(ang)
    fill = (LANES // 2 - HALF_ROPE, tm)
    cos = jnp.concatenate([cos_h, jnp.ones(fill, F32)] * 2, axis=0).T
    sin_signed = jnp.concatenate(
        [-sin_h, jnp.zeros(fill, F32), sin_h, jnp.zeros(fill, F32)], axis=0).T

    k = _dot(xn, wk_ref[...])
    for h in range(B_HEADS):
        sl = slice(h * B_HEAD_PAD, (h + 1) * B_HEAD_PAD)
        qbt_ref[h] = (_rope(qb[:, sl], cos, sin_signed) * B_QSCALE).T.astype(BF16)
    ka_ref[...] = k.astype(BF16)

    v = _dot(xn, wv_ref[...])
    store_transposed(qat_ref, q * A_QSCALE, N_PAIRS)
    kr = _rope(kr, cos, sin_signed)

    kn = _dot(ckvn, wkup_ref[...])
    store_transposed(vat_ref, v, N_PAIRS)
    vb = _dot(ckvn, wvup_ref[...])
    for h in range(B_HEADS):
        sl = slice(h * B_HEAD_PAD, (h + 1) * B_HEAD_PAD)
        kb_ref[:, sl] = (kn[:, sl] + kr).astype(BF16)
    store_transposed(vbt_ref, vb, N_PAIRS)


def _full(shape):
    return pl.BlockSpec(shape, lambda *_: (0,) * len(shape))


def _inproj(x2, pos3, g, wq, wk, wv, wf, wcq, wckv, wkr, bf, invf, gq, wqup, gkv,
            wkup, wvup, batch, seq):
    t = x2.shape[0]
    nt = seq // TM_IN
    row = lambda w: pl.BlockSpec((TM_IN, w), lambda b, s: (b * nt + s, 0))
    col = lambda n: pl.BlockSpec((None, n, LANES, TM_IN), lambda b, s: (b, 0, 0, s))
    col_shape = lambda n: jax.ShapeDtypeStruct((batch, n, LANES, seq), BF16)
    pos_spec = pl.BlockSpec((None, 1, TM_IN), lambda b, s: (b, 0, s))
    weights = (g, wq, wk, wv, wf, wcq, wckv, wkr, bf, invf, gq, wqup, gkv, wkup, wvup)
    return pl.pallas_call(
        _inproj_kernel,
        grid=(batch, nt),
        in_specs=[row(D_MODEL), pos_spec] + [_full(w.shape) for w in weights],
        out_specs=[
            col(N_PAIRS), row(A_WIDTH), col(N_PAIRS), row(LANES),
            col(B_HEADS), row(B_HEADS * B_HEAD_PAD), col(N_PAIRS),
        ],
        out_shape=[
            col_shape(N_PAIRS),
            jax.ShapeDtypeStruct((t, A_WIDTH), BF16),
            col_shape(N_PAIRS),
            jax.ShapeDtypeStruct((t, LANES), BF16),
            col_shape(B_HEADS),
            jax.ShapeDtypeStruct((t, B_HEADS * B_HEAD_PAD), BF16),
            col_shape(N_PAIRS),
        ],
        scratch_shapes=[pltpu.VMEM((A_HEADS, LANES), F32)],
        compiler_params=pltpu.CompilerParams(
            dimension_semantics=("arbitrary", "arbitrary"), vmem_limit_bytes=VMEM_LIMIT),
        name="inproj",
    )(x2, pos3, *weights)


def _attn_kernel(*refs, fox, nq):
    def tile(qi, carry):
        _attn_tile(qi, refs, fox, nq)
        return carry

    lax.fori_loop(0, nq, tile, 0)


def _attn_tile(qi, refs, fox, nq):
    if fox:
        qt_ref, k_ref, vt_ref, cb_ref, o_ref, m_scr, acc_scr, s_scr = refs
    else:
        qt_ref, k_ref, vt_ref, o_ref, m_scr, acc_scr, s_scr = refs
    head_dim = LANES // 2
    if fox:
        frow = lax.broadcasted_iota(jnp.int32, (LANES, TQ), 0)
        pair = pl.program_id(1)
        mine = [(frow < head_dim) == (h == 0) for h in range(2)]
        picks = [jnp.where((frow % A_HEADS == 2 * pair + h) & (frow < BIAS_PIECES * A_HEADS),
                           -1.0, 0.0).astype(BF16) for h in range(2)]

    def query_operands(tile):
        q0 = pl.multiple_of(tile * TQ, TQ)
        if fox:
            qt = qt_ref[:, pl.ds(q0, TQ)]
            return [jnp.concatenate([jnp.where(mine[h], qt, jnp.zeros_like(qt)), picks[h]], axis=0)
                    for h in range(2)]
        return [qt_ref[h, :, pl.ds(q0, TQ)] for h in range(2)]

    rhs = query_operands(qi)
    rhs_next = query_operands(jnp.minimum(qi + 1, nq - 1))

    m_scr[...] = jnp.full_like(m_scr, MASK_VALUE)
    acc_scr[...] = jnp.zeros_like(acc_scr)

    keys = lax.broadcasted_iota(jnp.int32, (TK, TQ), 0)
    queries = lax.broadcasted_iota(jnp.int32, (TK, TQ), 1)
    if fox:
        diag_mask = keys <= queries
    else:
        diag_mask = (keys // CHUNK) <= (queries // CHUNK)
    ones = jnp.ones((BF16_ROWS, TK), BF16)

    units = [(h, slice(c, c + QB)) for h in range(2) for c in range(0, TQ, QB)]

    def scores(j, h, cols, operands):
        start = pl.multiple_of(j * TK, TK)
        if fox:
            lhs = jnp.concatenate([k_ref[pl.ds(start, TK), :], cb_ref[pl.ds(start, TK), :]], axis=1)
        else:
            lhs = k_ref[pl.ds(start, TK), h * LANES:(h + 1) * LANES]
        return _dot(lhs, operands[h][:, cols])

    def update(j, slot, h, cols, masked):
        start = pl.multiple_of(j * TK, TK)
        s = s_scr[slot, h, :, cols]
        if masked:
            s = jnp.where(diag_mask[:, cols], s, MASK_VALUE)
        m_prev = m_scr[h, :, cols]
        m_new = jnp.maximum(m_prev, jnp.max(s, axis=0, keepdims=True))
        alpha = jnp.exp2(m_prev - m_new)
        p = jnp.exp2(s - m_new).astype(BF16)
        vt = jnp.concatenate(
            [vt_ref[h * head_dim:(h + 1) * head_dim, pl.ds(start, TK)], ones], axis=0)
        acc_scr[h, :, cols] = alpha * acc_scr[h, :, cols] + _dot(vt, p)
        m_scr[h, :, cols] = m_new

    first_slot = SCORE_SLOTS

    def stage(j, slot):
        nxt = 1 if slot == first_slot else (slot + 1) % SCORE_SLOTS
        for i in range(len(units) + 1):
            if i < len(units):
                h, cols = units[i]
                s_scr[nxt, h, :, cols] = scores(j + 1, h, cols, rhs)
            if i > 0:
                update(j, slot, *units[i - 1], False)

    def last_stage(j, slot):
        for h, cols in units:
            s_scr[first_slot, h, :, cols] = scores(0, h, cols, rhs_next)
            update(j, slot, h, cols, True)

    @pl.when(qi == 0)
    def _():
        for h, cols in units:
            s_scr[0, h, :, cols] = scores(0, h, cols, rhs)
        last_stage(0, 0)

    @pl.when(qi > 0)
    def _():
        stage(0, first_slot)

        def body(i, carry):
            for u in range(SCORE_SLOTS):
                stage(SCORE_SLOTS * i + 1 + u, (1 + u) % SCORE_SLOTS)
            return carry

        n_rounds = (qi - 1) // SCORE_SLOTS
        lax.fori_loop(0, n_rounds, body, 0)
        base = n_rounds * SCORE_SLOTS
        for r in range(SCORE_SLOTS):
            @pl.when(qi - 1 - base == r)
            def _(r=r):
                for u in range(r):
                    stage(base + 1 + u, (1 + u) % SCORE_SLOTS)
                last_stage(qi, (1 + r) % SCORE_SLOTS)

    out_t = jnp.concatenate(
        [acc_scr[h, :head_dim] / acc_scr[h, head_dim:head_dim + 1] for h in range(2)], axis=0)
    o_ref[pl.ds(pl.multiple_of(qi * TQ, TQ), TQ), :] = out_t.T.astype(o_ref.dtype)


def _attention(qt, k, vt, cb, batch, seq, fox):
    t = k.shape[0]
    nq = seq // TQ
    assert TQ == TK
    qt_spec = pl.BlockSpec((None, None if fox else 2, LANES, seq), lambda b, p: (b, p, 0, 0))
    in_specs = [
        qt_spec,
        pl.BlockSpec((seq, LANES if fox else 2 * LANES), lambda b, p: (b, p)),
        pl.BlockSpec((None, None, LANES, seq), lambda b, p: (b, p, 0, 0)),
    ]
    args = [qt, k, vt]
    if fox:
        in_specs.append(pl.BlockSpec((seq, LANES), lambda b, p: (b, 0)))
        args.append(cb)
    acc_rows = LANES // 2 + BF16_ROWS
    return pl.pallas_call(
        functools.partial(_attn_kernel, fox=fox, nq=nq),
        grid=(batch, N_PAIRS),
        in_specs=in_specs,
        out_specs=pl.BlockSpec((seq, LANES), lambda b, p: (b, p)),
        out_shape=jax.ShapeDtypeStruct((t, A_WIDTH), BF16),
        scratch_shapes=[pltpu.VMEM((2, 1, TQ), F32), pltpu.VMEM((2, acc_rows, TQ), F32),
                        pltpu.VMEM((SCORE_SLOTS + 1, 2, TK, TQ), F32)],
        compiler_params=pltpu.CompilerParams(
            dimension_semantics=("arbitrary", "arbitrary"),
            vmem_limit_bytes=VMEM_LIMIT),
        name="fox_attention" if fox else "mla_attention",
    )(*args)


def _post_kernel(x_ref, ya_ref, yb_ref, g_ref, wga_ref, wgb_ref, bga_ref, bgb_ref, wa_ref,
                 wb_ref, wout_ref, gffn_ref, wgate_ref, wup_ref, wdown_ref, gfin_ref,
                 o_ref, act_scr):
    half = x_ref.shape[0] // 2
    ra, rb = slice(0, half), slice(half, 2 * half)

    def norm_in(rows):
        return _rms(x_ref[rows, :], g_ref[...]).astype(BF16)

    def gate(xn, w_ref, b_ref):
        return jax.nn.sigmoid(_dot(xn, w_ref[...]) + b_ref[...])

    def merge_out(rows, gate_a, gate_b):
        merged = (gate_a * _dot(ya_ref[rows, :], wa_ref[...])
                  + gate_b * _dot(yb_ref[rows, :], wb_ref[...]))
        return x_ref[rows, :] + _dot(merged.astype(BF16), wout_ref[...])

    def ffn_chunk(rows, hn, c):
        sl = slice(c, c + FFN_CHUNK)
        gt = _dot(hn, wgate_ref[:, sl])
        act_scr[rows, sl] = (gt * jax.nn.sigmoid(gt) * _dot(hn, wup_ref[:, sl])).astype(BF16)

    chunks = range(0, FFN_HIDDEN, FFN_CHUNK)

    xn_a = norm_in(ra)
    gate_aa = gate(xn_a, wga_ref, bga_ref)
    xn_b = norm_in(rb)
    h_a = merge_out(ra, gate_aa, gate(xn_a, wgb_ref, bgb_ref))
    gate_ab = gate(xn_b, wga_ref, bga_ref)
    hn_a = _rms(h_a, gffn_ref[...]).astype(BF16)
    h_b = merge_out(rb, gate_ab, gate(xn_b, wgb_ref, bgb_ref))
    ffn_chunk(ra, hn_a, chunks[0])
    hn_b = _rms(h_b, gffn_ref[...]).astype(BF16)
    for c in chunks[1:]:
        ffn_chunk(ra, hn_a, c)
    h_a = h_a + _dot(act_scr[ra, :], wdown_ref[...])
    ffn_chunk(rb, hn_b, chunks[0])
    o_ref[ra, :] = _rms(h_a, gfin_ref[...])
    for c in chunks[1:]:
        ffn_chunk(rb, hn_b, c)
    h_b = h_b + _dot(act_scr[rb, :], wdown_ref[...])
    o_ref[rb, :] = _rms(h_b, gfin_ref[...])


def _post(x2, ya, yb, *weights):
    t = x2.shape[0]
    row = lambda w: pl.BlockSpec((TM_OUT, w), lambda i: (i, 0))
    const = lambda w: pl.BlockSpec(w.shape, lambda i: (0,) * w.ndim,
                                   pipeline_mode=pl.Buffered(1))
    return pl.pallas_call(
        _post_kernel,
        grid=(t // TM_OUT,),
        in_specs=[row(D_MODEL), row(A_WIDTH), row(B_WIDTH)] + [const(w) for w in weights],
        out_specs=row(D_MODEL),
        out_shape=jax.ShapeDtypeStruct((t, D_MODEL), F32),
        scratch_shapes=[pltpu.VMEM((TM_OUT, FFN_HIDDEN), BF16)],
        compiler_params=pltpu.CompilerParams(
            dimension_semantics=("arbitrary",), vmem_limit_bytes=VMEM_LIMIT),
        name="merge_ffn",
    )(x2, ya, yb, *weights)


def _mla_lane_layout():
    half = LANES // 2
    nope = np.concatenate([np.arange(HALF_ROPE, half),
                           np.arange(half + HALF_ROPE, B_NOPE + 2 * HALF_ROPE)])
    rope = np.concatenate([np.arange(HALF_ROPE), half + np.arange(HALF_ROPE)])
    return nope, rope


def kernel(x, positions, norm_mix_g, w_in, b_forget, q_a_norm_g, w_q_up, kv_a_norm_g, w_kv_up,
           w_branch_a, w_branch_b, b_gate, w_out, norm_ffn_g, w_ffn_gate, w_ffn_up, w_ffn_down,
           norm_final_g):
    batch, seq, _ = x.shape
    t = batch * seq
    assert norm_mix_g.shape[0] == 1 and seq % TM_IN == 0 and seq % TQ == 0 and t % TM_OUT == 0
    x2 = x.reshape(t, D_MODEL)
    pos3 = positions.reshape(batch, 1, seq)
    nope_lane, rope_lane = _mla_lane_layout()

    w = w_in[0].astype(BF16)
    o = np.cumsum((0, A_WIDTH, A_WIDTH, A_WIDTH, A_HEADS, B_Q_RANK, B_KV_RANK, B_ROPE,
                   D_MODEL, D_MODEL))
    wq, wk, wv = (w[:, o[i]:o[i + 1]] for i in range(3))
    wf = jnp.zeros((D_MODEL, LANES), BF16).at[:, :A_HEADS].set(w[:, o[3]:o[4]])
    wcq, wckv = w[:, o[4]:o[5]], w[:, o[5]:o[6]]
    wkr = jnp.zeros((D_MODEL, LANES), BF16).at[:, rope_lane].set(w[:, o[6]:o[7]])
    wga, wgb = w[:, o[7]:o[8]], w[:, o[8]:o[9]]

    wqu = w_q_up[0].astype(BF16).reshape(B_Q_RANK, B_HEADS, B_NOPE + B_ROPE)
    wqup = jnp.zeros((B_Q_RANK, B_HEADS, B_HEAD_PAD), BF16)
    wqup = wqup.at[:, :, nope_lane].set(wqu[:, :, :B_NOPE]).at[:, :, rope_lane].set(wqu[:, :, B_NOPE:])
    wqup = wqup.reshape(B_Q_RANK, B_HEADS * B_HEAD_PAD)
    wkvu = w_kv_up[0].astype(BF16).reshape(B_KV_RANK, B_HEADS, B_NOPE + B_V_DIM)
    wkup = jnp.zeros((B_KV_RANK, B_HEADS, B_HEAD_PAD), BF16).at[:, :, nope_lane].set(wkvu[:, :, :B_NOPE])
    wkup = wkup.reshape(B_KV_RANK, B_HEADS * B_HEAD_PAD)
    wvup = wkvu[:, :, B_NOPE:].reshape(B_KV_RANK, B_WIDTH)

    invf = (ROPE_BASE ** (-jnp.arange(HALF_ROPE, dtype=F32) / HALF_ROPE)).reshape(HALF_ROPE, 1)
    bf = jnp.broadcast_to(b_forget[0].astype(F32)[:, None], (A_HEADS, TM_IN))
    row = lambda v: v.astype(F32).reshape(1, -1)

    qat, ka, vat, cb, qbt, kb, vbt = _inproj(
        x2, pos3, row(norm_mix_g[0]), wq, wk, wv, wf, wcq, wckv, wkr, bf, invf,
        row(q_a_norm_g[0]), wqup, row(kv_a_norm_g[0]), wkup, wvup, batch, seq)

    ya = _attention(qat, ka, vat, cb, batch, seq, fox=True)
    yb = _attention(qbt, kb, vbt, None, batch, seq, fox=False)

    out = _post(
        x2, ya, yb, row(norm_mix_g[0]), wga, wgb, row(b_gate[0, :D_MODEL]), row(b_gate[0, D_MODEL:]),
        w_branch_a[0].astype(BF16), w_branch_b[0].astype(BF16), w_out[0].astype(BF16),
        row(norm_ffn_g[0]), w_ffn_gate[0].astype(BF16), w_ffn_up[0].astype(BF16),
        w_ffn_down[0].astype(BF16), row(norm_final_g))
    return out.reshape(batch, seq, D_MODEL)
```

```python
import functools

import jax
import jax.numpy as jnp
import numpy as np
from jax import lax
from jax.experimental import pallas as pl
from jax.experimental.pallas import tpu as pltpu

D_MODEL = 1024
CHUNK = 64
A_HEADS = 8
A_HEAD_DIM = 64
A_WIDTH = A_HEADS * A_HEAD_DIM
B_HEADS = 8
B_Q_RANK = 256
B_KV_RANK = 128
B_NOPE = 64
B_ROPE = 32
B_V_DIM = 64
B_WIDTH = B_HEADS * B_V_DIM
ROPE_BASE = 10000.0
FFN_HIDDEN = 2816
NORM_EPS = 1e-6
MASK_VALUE = -1e30

LANES = 128
BF16_ROWS = 16
LOG2E = 1.4426950408889634
A_QSCALE = (A_HEAD_DIM ** -0.5) * LOG2E
B_QSCALE = ((B_NOPE + B_ROPE) ** -0.5) * LOG2E
HALF_ROPE = B_ROPE // 2
B_HEAD_PAD = LANES
N_PAIRS = A_HEADS // 2
BIAS_PIECES = 3

TM_IN = 512
TM_OUT = 512
TQ = 512
TK = 512
SCORE_SLOTS = 3
ROUND_STAGES = 6
QB = 256
FFN_CHUNK = 256
VMEM_LIMIT = 56 * 1024 * 1024

F32 = jnp.float32
BF16 = jnp.bfloat16


def _rms(x, g):
    ms = jnp.mean(x * x, axis=-1, keepdims=True)
    return x * lax.rsqrt(ms + NORM_EPS) * g


def _dot(a, b):
    return jnp.dot(a, b, preferred_element_type=F32)


def _rope(t, cos, sin_signed):
    return t * cos + pltpu.roll(t, LANES // 2, axis=1) * sin_signed


def _inproj_kernel(x_ref, pos_ref, g_ref, wq_ref, wk_ref, wv_ref, wf_ref, wcq_ref,
                   wckv_ref, wkr_ref, bf_ref, invf_ref, gq_ref, wqup_ref,
                   gkv_ref, wkup_ref, wvup_ref,
                   qat_ref, ka_ref, vat_ref, cb_ref, qbt_ref, kb_ref, vbt_ref, carry_ref):
    @pl.when(pl.program_id(1) == 0)
    def _():
        carry_ref[...] = jnp.zeros_like(carry_ref)

    tm = x_ref.shape[0]
    xn = _rms(x_ref[...], g_ref[...]).astype(BF16)

    def store_transposed(ref, val, groups):
        for i in range(groups):
            ref[i] = val[:, i * LANES:(i + 1) * LANES].T.astype(BF16)

    f = _dot(xn, wf_ref[...])
    cq = _dot(xn, wcq_ref[...])
    ckv = _dot(xn, wckv_ref[...])
    kr = _dot(xn, wkr_ref[...])
    cqn = _rms(cq, gq_ref[...]).astype(BF16)
    ckvn = _rms(ckv, gkv_ref[...]).astype(BF16)
    qb = _dot(cqn, wqup_ref[...])
    q = _dot(xn, wq_ref[...])

    z = f.T[:A_HEADS] + bf_ref[...]
    c = jnp.minimum(z, 0.0) - jnp.log1p(jnp.exp(-jnp.abs(z)))
    lane = lax.broadcasted_iota(jnp.int32, c.shape, 1)
    d = 1
    while d < tm:
        c = c + jnp.where(lane >= d, pltpu.roll(c, d, axis=1), 0.0)
        d *= 2
    c = c + jnp.tile(carry_ref[...], (1, tm // LANES))
    carry_ref[...] = jnp.broadcast_to(c[:, tm - 1:tm], carry_ref.shape)
    c = c * LOG2E
    hi = c.astype(BF16).astype(F32)
    mid = (c - hi).astype(BF16).astype(F32)
    lo = c - hi - mid
    pad = jnp.zeros((LANES - BIAS_PIECES * A_HEADS, tm), F32)
    cb_ref[...] = jnp.concatenate([hi, mid, lo, pad], axis=0).T.astype(BF16)

    ang = invf_ref[...] * pos_ref[...].astype(F32)
    cos_h, sin_h = jnp.cos(ang), jnp.sin(ang)
    fill = (LANES // 2 - HALF_ROPE, tm)
    cos = jnp.concatenate([cos_h, jnp.ones(fill, F32)] * 2, axis=0).T
    sin_signed = jnp.concatenate(
        [-sin_h, jnp.zeros(fill, F32), sin_h, jnp.zeros(fill, F32)], axis=0).T

    k = _dot(xn, wk_ref[...])
    for h in range(B_HEADS):
        sl = slice(h * B_HEAD_PAD, (h + 1) * B_HEAD_PAD)
        qbt_ref[h] = (_rope(qb[:, sl], cos, sin_signed) * B_QSCALE).T.astype(BF16)
    ka_ref[...] = k.astype(BF16)

    v = _dot(xn, wv_ref[...])
    store_transposed(qat_ref, q * A_QSCALE, N_PAIRS)
    kr = _rope(kr, cos, sin_signed)

    kn = _dot(ckvn, wkup_ref[...])
    store_transposed(vat_ref, v, N_PAIRS)
    vb = _dot(ckvn, wvup_ref[...])
    for h in range(B_HEADS):
        sl = slice(h * B_HEAD_PAD, (h + 1) * B_HEAD_PAD)
        kb_ref[:, sl] = (kn[:, sl] + kr).astype(BF16)
    store_transposed(vbt_ref, vb, N_PAIRS)


def _full(shape):
    return pl.BlockSpec(shape, lambda *_: (0,) * len(shape))


def _inproj(x2, pos3, g, wq, wk, wv, wf, wcq, wckv, wkr, bf, invf, gq, wqup, gkv,
            wkup, wvup, batch, seq):
    t = x2.shape[0]
    nt = seq // TM_IN
    row = lambda w: pl.BlockSpec((TM_IN, w), lambda b, s: (b * nt + s, 0))
    col = lambda n: pl.BlockSpec((None, n, LANES, TM_IN), lambda b, s: (b, 0, 0, s))
    col_shape = lambda n: jax.ShapeDtypeStruct((batch, n, LANES, seq), BF16)
    pos_spec = pl.BlockSpec((None, 1, TM_IN), lambda b, s: (b, 0, s))
    weights = (g, wq, wk, wv, wf, wcq, wckv, wkr, bf, invf, gq, wqup, gkv, wkup, wvup)
    return pl.pallas_call(
        _inproj_kernel,
        grid=(batch, nt),
        in_specs=[row(D_MODEL), pos_spec] + [_full(w.shape) for w in weights],
        out_specs=[
            col(N_PAIRS), row(A_WIDTH), col(N_PAIRS), row(LANES),
            col(B_HEADS), row(B_HEADS * B_HEAD_PAD), col(N_PAIRS),
        ],
        out_shape=[
            col_shape(N_PAIRS),
            jax.ShapeDtypeStruct((t, A_WIDTH), BF16),
            col_shape(N_PAIRS),
            jax.ShapeDtypeStruct((t, LANES), BF16),
            col_shape(B_HEADS),
            jax.ShapeDtypeStruct((t, B_HEADS * B_HEAD_PAD), BF16),
            col_shape(N_PAIRS),
        ],
        scratch_shapes=[pltpu.VMEM((A_HEADS, LANES), F32)],
        compiler_params=pltpu.CompilerParams(
            dimension_semantics=("arbitrary", "arbitrary"), vmem_limit_bytes=VMEM_LIMIT),
        name="inproj",
    )(x2, pos3, *weights)


def _attn_kernel(*refs, fox, nq):
    def tile(qi, carry):
        _attn_tile(qi, refs, fox, nq)
        return carry

    lax.fori_loop(0, nq, tile, 0)


def _attn_tile(qi, refs, fox, nq):
    if fox:
        qt_ref, k_ref, vt_ref, cb_ref, o_ref, m_scr, acc_scr, s_scr = refs
    else:
        qt_ref, k_ref, vt_ref, o_ref, m_scr, acc_scr, s_scr = refs
    head_dim = LANES // 2
    if fox:
        frow = lax.broadcasted_iota(jnp.int32, (LANES, TQ), 0)
        pair = pl.program_id(1)
        mine = [(frow < head_dim) == (h == 0) for h in range(2)]
        picks = [jnp.where((frow % A_HEADS == 2 * pair + h) & (frow < BIAS_PIECES * A_HEADS),
                           -1.0, 0.0).astype(BF16) for h in range(2)]

    def query_operands(tile):
        q0 = pl.multiple_of(tile * TQ, TQ)
        if fox:
            qt = qt_ref[:, pl.ds(q0, TQ)]
            return [jnp.concatenate([jnp.where(mine[h], qt, jnp.zeros_like(qt)), picks[h]], axis=0)
                    for h in range(2)]
        return [qt_ref[h, :, pl.ds(q0, TQ)] for h in range(2)]

    rhs = query_operands(qi)
    rhs_next = query_operands(jnp.minimum(qi + 1, nq - 1))

    m_scr[...] = jnp.full_like(m_scr, MASK_VALUE)
    acc_scr[...] = jnp.zeros_like(acc_scr)

    keys = lax.broadcasted_iota(jnp.int32, (TK, TQ), 0)
    queries = lax.broadcasted_iota(jnp.int32, (TK, TQ), 1)
    if fox:
        diag_mask = keys <= queries
    else:
        diag_mask = (keys // CHUNK) <= (queries // CHUNK)
    ones = jnp.ones((BF16_ROWS, TK), BF16)

    units = [(h, slice(c, c + QB)) for h in range(2) for c in range(0, TQ, QB)]

    def scores(j, h, cols, operands):
        start = pl.multiple_of(j * TK, TK)
        if fox:
            lhs = jnp.concatenate([k_ref[pl.ds(start, TK), :], cb_ref[pl.ds(start, TK), :]], axis=1)
        else:
            lhs = k_ref[pl.ds(start, TK), h * LANES:(h + 1) * LANES]
        return _dot(lhs, operands[h][:, cols])

    def update(j, slot, h, cols, masked):
        start = pl.multiple_of(j * TK, TK)
        s = s_scr[slot, h, :, cols]
        if masked:
            s = jnp.where(diag_mask[:, cols], s, MASK_VALUE)
        m_prev = m_scr[h, :, cols]
        m_new = jnp.maximum(m_prev, jnp.max(s, axis=0, keepdims=True))
        alpha = jnp.exp2(m_prev - m_new)
        p = jnp.exp2(s - m_new).astype(BF16)
        vt = jnp.concatenate(
            [vt_ref[h * head_dim:(h + 1) * head_dim, pl.ds(start, TK)], ones], axis=0)
        acc_scr[h, :, cols] = alpha * acc_scr[h, :, cols] + _dot(vt, p)
        m_scr[h, :, cols] = m_new

    def stage(j, slot):
        nxt = (slot + 1) % SCORE_SLOTS
        for i in range(len(units) + 1):
            if i < len(units):
                h, cols = units[i]
                s_scr[nxt, h, :, cols] = scores(j + 1, h, cols, rhs)
            if i > 0:
                update(j, slot, *units[i - 1], False)

    def last_stage(j, slot):
        for h, cols in units:
            if slot == 0:
                update(j, slot, h, cols, True)
            s_scr[0, h, :, cols] = scores(0, h, cols, rhs_next)
            if slot != 0:
                update(j, slot, h, cols, True)

    @pl.when(qi == 0)
    def _():
        for h, cols in units:
            s_scr[0, h, :, cols] = scores(0, h, cols, rhs)

    def body(i, carry):
        for u in range(ROUND_STAGES):
            stage(ROUND_STAGES * i + u, u % SCORE_SLOTS)
        return carry

    n_rounds = qi // ROUND_STAGES
    lax.fori_loop(0, n_rounds, body, 0)
    base = n_rounds * ROUND_STAGES
    for r in range(ROUND_STAGES):
        @pl.when(qi - base == r)
        def _(r=r):
            for u in range(r):
                stage(base + u, u % SCORE_SLOTS)
            last_stage(qi, r % SCORE_SLOTS)

    out_t = jnp.concatenate(
        [acc_scr[h, :head_dim] / acc_scr[h, head_dim:head_dim + 1] for h in range(2)], axis=0)
    o_ref[pl.ds(pl.multiple_of(qi * TQ, TQ), TQ), :] = out_t.T.astype(o_ref.dtype)


def _attention(qt, k, vt, cb, batch, seq, fox):
    t = k.shape[0]
    nq = seq // TQ
    assert TQ == TK
    qt_spec = pl.BlockSpec((None, None if fox else 2, LANES, seq), lambda b, p: (b, p, 0, 0))
    in_specs = [
        qt_spec,
        pl.BlockSpec((seq, LANES if fox else 2 * LANES), lambda b, p: (b, p)),
        pl.BlockSpec((None, None, LANES, seq), lambda b, p: (b, p, 0, 0)),
    ]
    args = [qt, k, vt]
    if fox:
        in_specs.append(pl.BlockSpec((seq, LANES), lambda b, p: (b, 0)))
        args.append(cb)
    acc_rows = LANES // 2 + BF16_ROWS
    return pl.pallas_call(
        functools.partial(_attn_kernel, fox=fox, nq=nq),
        grid=(batch, N_PAIRS),
        in_specs=in_specs,
        out_specs=pl.BlockSpec((seq, LANES), lambda b, p: (b, p)),
        out_shape=jax.ShapeDtypeStruct((t, A_WIDTH), BF16),
        scratch_shapes=[pltpu.VMEM((2, 1, TQ), F32), pltpu.VMEM((2, acc_rows, TQ), F32),
                        pltpu.VMEM((SCORE_SLOTS, 2, TK, TQ), F32)],
        compiler_params=pltpu.CompilerParams(
            dimension_semantics=("arbitrary", "arbitrary"),
            vmem_limit_bytes=VMEM_LIMIT),
        name="fox_attention" if fox else "mla_attention",
    )(*args)


def _post_kernel(x_ref, ya_ref, yb_ref, g_ref, wga_ref, wgb_ref, bga_ref, bgb_ref, wa_ref,
                 wb_ref, wout_ref, gffn_ref, wgate_ref, wup_ref, wdown_ref, gfin_ref,
                 o_ref, act_scr):
    half = x_ref.shape[0] // 2
    ra, rb = slice(0, half), slice(half, 2 * half)

    def norm_in(rows):
        return _rms(x_ref[rows, :], g_ref[...]).astype(BF16)

    def gate(xn, w_ref, b_ref):
        return jax.nn.sigmoid(_dot(xn, w_ref[...]) + b_ref[...])

    def merge_out(rows, gate_a, gate_b):
        merged = (gate_a * _dot(ya_ref[rows, :], wa_ref[...])
                  + gate_b * _dot(yb_ref[rows, :], wb_ref[...]))
        return x_ref[rows, :] + _dot(merged.astype(BF16), wout_ref[...])

    def ffn_chunk(rows, hn, c):
        sl = slice(c, c + FFN_CHUNK)
        gt = _dot(hn, wgate_ref[:, sl])
        act_scr[rows, sl] = (gt * jax.nn.sigmoid(gt) * _dot(hn, wup_ref[:, sl])).astype(BF16)

    chunks = range(0, FFN_HIDDEN, FFN_CHUNK)

    xn_a = norm_in(ra)
    gate_aa = gate(xn_a, wga_ref, bga_ref)
    xn_b = norm_in(rb)
    h_a = merge_out(ra, gate_aa, gate(xn_a, wgb_ref, bgb_ref))
    gate_ab = gate(xn_b, wga_ref, bga_ref)
    hn_a = _rms(h_a, gffn_ref[...]).astype(BF16)
    h_b = merge_out(rb, gate_ab, gate(xn_b, wgb_ref, bgb_ref))
    ffn_chunk(ra, hn_a, chunks[0])
    hn_b = _rms(h_b, gffn_ref[...]).astype(BF16)
    for c in chunks[1:]:
        ffn_chunk(ra, hn_a, c)
    h_a = h_a + _dot(act_scr[ra, :], wdown_ref[...])
    ffn_chunk(rb, hn_b, chunks[0])
    o_ref[ra, :] = _rms(h_a, gfin_ref[...])
    for c in chunks[1:]:
        ffn_chunk(rb, hn_b, c)
    h_b = h_b + _dot(act_scr[rb, :], wdown_ref[...])
    o_ref[rb, :] = _rms(h_b, gfin_ref[...])


def _post(x2, ya, yb, *weights):
    t = x2.shape[0]
    row = lambda w: pl.BlockSpec((TM_OUT, w), lambda i: (i, 0))
    const = lambda w: pl.BlockSpec(w.shape, lambda i: (0,) * w.ndim,
                                   pipeline_mode=pl.Buffered(1))
    return pl.pallas_call(
        _post_kernel,
        grid=(t // TM_OUT,),
        in_specs=[row(D_MODEL), row(A_WIDTH), row(B_WIDTH)] + [const(w) for w in weights],
        out_specs=row(D_MODEL),
        out_shape=jax.ShapeDtypeStruct((t, D_MODEL), F32),
        scratch_shapes=[pltpu.VMEM((TM_OUT, FFN_HIDDEN), BF16)],
        compiler_params=pltpu.CompilerParams(
            dimension_semantics=("arbitrary",), vmem_limit_bytes=VMEM_LIMIT),
        name="merge_ffn",
    )(x2, ya, yb, *weights)


def _mla_lane_layout():
    half = LANES // 2
    nope = np.concatenate([np.arange(HALF_ROPE, half),
                           np.arange(half + HALF_ROPE, B_NOPE + 2 * HALF_ROPE)])
    rope = np.concatenate([np.arange(HALF_ROPE), half + np.arange(HALF_ROPE)])
    return nope, rope


def kernel(x, positions, norm_mix_g, w_in, b_forget, q_a_norm_g, w_q_up, kv_a_norm_g, w_kv_up,
           w_branch_a, w_branch_b, b_gate, w_out, norm_ffn_g, w_ffn_gate, w_ffn_up, w_ffn_down,
           norm_final_g):
    batch, seq, _ = x.shape
    t = batch * seq
    assert norm_mix_g.shape[0] == 1 and seq % TM_IN == 0 and seq % TQ == 0 and t % TM_OUT == 0
    x2 = x.reshape(t, D_MODEL)
    pos3 = positions.reshape(batch, 1, seq)
    nope_lane, rope_lane = _mla_lane_layout()

    w = w_in[0].astype(BF16)
    o = np.cumsum((0, A_WIDTH, A_WIDTH, A_WIDTH, A_HEADS, B_Q_RANK, B_KV_RANK, B_ROPE,
                   D_MODEL, D_MODEL))
    wq, wk, wv = (w[:, o[i]:o[i + 1]] for i in range(3))
    wf = jnp.zeros((D_MODEL, LANES), BF16).at[:, :A_HEADS].set(w[:, o[3]:o[4]])
    wcq, wckv = w[:, o[4]:o[5]], w[:, o[5]:o[6]]
    wkr = jnp.zeros((D_MODEL, LANES), BF16).at[:, rope_lane].set(w[:, o[6]:o[7]])
    wga, wgb = w[:, o[7]:o[8]], w[:, o[8]:o[9]]

    wqu = w_q_up[0].astype(BF16).reshape(B_Q_RANK, B_HEADS, B_NOPE + B_ROPE)
    wqup = jnp.zeros((B_Q_RANK, B_HEADS, B_HEAD_PAD), BF16)
    wqup = wqup.at[:, :, nope_lane].set(wqu[:, :, :B_NOPE]).at[:, :, rope_lane].set(wqu[:, :, B_NOPE:])
    wqup = wqup.reshape(B_Q_RANK, B_HEADS * B_HEAD_PAD)
    wkvu = w_kv_up[0].astype(BF16).reshape(B_KV_RANK, B_HEADS, B_NOPE + B_V_DIM)
    wkup = jnp.zeros((B_KV_RANK, B_HEADS, B_HEAD_PAD), BF16).at[:, :, nope_lane].set(wkvu[:, :, :B_NOPE])
    wkup = wkup.reshape(B_KV_RANK, B_HEADS * B_HEAD_PAD)
    wvup = wkvu[:, :, B_NOPE:].reshape(B_KV_RANK, B_WIDTH)

    invf = (ROPE_BASE ** (-jnp.arange(HALF_ROPE, dtype=F32) / HALF_ROPE)).reshape(HALF_ROPE, 1)
    bf = jnp.broadcast_to(b_forget[0].astype(F32)[:, None], (A_HEADS, TM_IN))
    row = lambda v: v.astype(F32).reshape(1, -1)

    qat, ka, vat, cb, qbt, kb, vbt = _inproj(
        x2, pos3, row(norm_mix_g[0]), wq, wk, wv, wf, wcq, wckv, wkr, bf, invf,
        row(q_a_norm_g[0]), wqup, row(kv_a_norm_g[0]), wkup, wvup, batch, seq)

    ya = _attention(qat, ka, vat, cb, batch, seq, fox=True)
    yb = _attention(qbt, kb, vbt, None, batch, seq, fox=False)

    out = _post(
        x2, ya, yb, row(norm_mix_g[0]), wga, wgb, row(b_gate[0, :D_MODEL]), row(b_gate[0, D_MODEL:]),
        w_branch_a[0].astype(BF16), w_branch_b[0].astype(BF16), w_out[0].astype(BF16),
        row(norm_ffn_g[0]), w_ffn_gate[0].astype(BF16), w_ffn_up[0].astype(BF16),
        w_ffn_down[0].astype(BF16), row(norm_final_g))
    return out.reshape(batch, seq, D_MODEL)
```

```python
import functools

import jax
import jax.numpy as jnp
import numpy as np
from jax import lax
from jax.experimental import pallas as pl
from jax.experimental.pallas import tpu as pltpu

D_MODEL = 1024
CHUNK = 64
A_HEADS = 8
A_HEAD_DIM = 64
A_WIDTH = A_HEADS * A_HEAD_DIM
B_HEADS = 8
B_Q_RANK = 256
B_KV_RANK = 128
B_NOPE = 64
B_ROPE = 32
B_V_DIM = 64
B_WIDTH = B_HEADS * B_V_DIM
ROPE_BASE = 10000.0
FFN_HIDDEN = 2816
NORM_EPS = 1e-6
MASK_VALUE = -1e30

LANES = 128
BF16_ROWS = 16
LOG2E = 1.4426950408889634
A_QSCALE = (A_HEAD_DIM ** -0.5) * LOG2E
B_QSCALE = ((B_NOPE + B_ROPE) ** -0.5) * LOG2E
HALF_ROPE = B_ROPE // 2
B_HEAD_PAD = LANES
N_PAIRS = A_HEADS // 2
BIAS_PIECES = 3

TM_IN = 512
TM_OUT = 512
TQ = 512
TK = 512
SCORE_SLOTS = 3
ROUND_STAGES = 6
QB = 256
FFN_CHUNK = 256
VMEM_LIMIT = 56 * 1024 * 1024

F32 = jnp.float32
BF16 = jnp.bfloat16


def _rms(x, g):
    ms = jnp.mean(x * x, axis=-1, keepdims=True)
    return x * lax.rsqrt(ms + NORM_EPS) * g


def _dot(a, b):
    return jnp.dot(a, b, preferred_element_type=F32)


def _rope(t, cos, sin_signed):
    return t * cos + pltpu.roll(t, LANES // 2, axis=1) * sin_signed


def _inproj_kernel(x_ref, pos_ref, g_ref, wq_ref, wk_ref, wv_ref, wf_ref, wcq_ref,
                   wckv_ref, wkr_ref, bf_ref, invf_ref, gq_ref, wqup_ref,
                   gkv_ref, wkup_ref, wvup_ref,
                   qat_ref, ka_ref, vat_ref, cb_ref, qbt_ref, kb_ref, vbt_ref, carry_ref):
    @pl.when(pl.program_id(1) == 0)
    def _():
        carry_ref[...] = jnp.zeros_like(carry_ref)

    tm = x_ref.shape[0]
    xn = _rms(x_ref[...], g_ref[...]).astype(BF16)

    def store_transposed(ref, val, groups):
        for i in range(groups):
            ref[i] = val[:, i * LANES:(i + 1) * LANES].T.astype(BF16)

    f = _dot(xn, wf_ref[...])
    cq = _dot(xn, wcq_ref[...])
    ckv = _dot(xn, wckv_ref[...])
    kr = _dot(xn, wkr_ref[...])
    cqn = _rms(cq, gq_ref[...]).astype(BF16)
    ckvn = _rms(ckv, gkv_ref[...]).astype(BF16)
    qb = _dot(cqn, wqup_ref[...])
    q = _dot(xn, wq_ref[...])

    z = f.T[:A_HEADS] + bf_ref[...]
    c = jnp.minimum(z, 0.0) - jnp.log1p(jnp.exp(-jnp.abs(z)))
    lane = lax.broadcasted_iota(jnp.int32, c.shape, 1)
    d = 1
    while d < tm:
        c = c + jnp.where(lane >= d, pltpu.roll(c, d, axis=1), 0.0)
        d *= 2
    c = c + jnp.tile(carry_ref[...], (1, tm // LANES))
    carry_ref[...] = jnp.broadcast_to(c[:, tm - 1:tm], carry_ref.shape)
    c = c * LOG2E
    hi = c.astype(BF16).astype(F32)
    mid = (c - hi).astype(BF16).astype(F32)
    lo = c - hi - mid
    pad = jnp.zeros((LANES - BIAS_PIECES * A_HEADS, tm), F32)
    cb_ref[...] = jnp.concatenate([hi, mid, lo, pad], axis=0).T.astype(BF16)

    ang = invf_ref[...] * pos_ref[...].astype(F32)
    cos_h, sin_h = jnp.cos(ang), jnp.sin(ang)
    fill = (LANES // 2 - HALF_ROPE, tm)
    cos = jnp.concatenate([cos_h, jnp.ones(fill, F32)] * 2, axis=0).T
    sin_signed = jnp.concatenate(
        [-sin_h, jnp.zeros(fill, F32), sin_h, jnp.zeros(fill, F32)], axis=0).T

    k = _dot(xn, wk_ref[...])
    for h in range(B_HEADS):
        sl = slice(h * B_HEAD_PAD, (h + 1) * B_HEAD_PAD)
        qbt_ref[h] = (_rope(qb[:, sl], cos, sin_signed) * B_QSCALE).T.astype(BF16)
    ka_ref[...] = k.astype(BF16)

    v = _dot(xn, wv_ref[...])
    store_transposed(qat_ref, q * A_QSCALE, N_PAIRS)
    kr = _rope(kr, cos, sin_signed)

    kn = _dot(ckvn, wkup_ref[...])
    store_transposed(vat_ref, v, N_PAIRS)
    vb = _dot(ckvn, wvup_ref[...])
    for h in range(B_HEADS):
        sl = slice(h * B_HEAD_PAD, (h + 1) * B_HEAD_PAD)
        kb_ref[:, sl] = (kn[:, sl] + kr).astype(BF16)
    store_transposed(vbt_ref, vb, N_PAIRS)


def _full(shape):
    return pl.BlockSpec(shape, lambda *_: (0,) * len(shape))


def _inproj(x2, pos3, g, wq, wk, wv, wf, wcq, wckv, wkr, bf, invf, gq, wqup, gkv,
            wkup, wvup, batch, seq):
    t = x2.shape[0]
    nt = seq // TM_IN
    row = lambda w: pl.BlockSpec((TM_IN, w), lambda b, s: (b * nt + s, 0))
    col = lambda n: pl.BlockSpec((None, n, LANES, TM_IN), lambda b, s: (b, 0, 0, s))
    col_shape = lambda n: jax.ShapeDtypeStruct((batch, n, LANES, seq), BF16)
    pos_spec = pl.BlockSpec((None, 1, TM_IN), lambda b, s: (b, 0, s))
    weights = (g, wq, wk, wv, wf, wcq, wckv, wkr, bf, invf, gq, wqup, gkv, wkup, wvup)
    return pl.pallas_call(
        _inproj_kernel,
        grid=(batch, nt),
        in_specs=[row(D_MODEL), pos_spec] + [_full(w.shape) for w in weights],
        out_specs=[
            col(N_PAIRS), row(A_WIDTH), col(N_PAIRS), row(LANES),
            col(B_HEADS), row(B_HEADS * B_HEAD_PAD), col(N_PAIRS),
        ],
        out_shape=[
            col_shape(N_PAIRS),
            jax.ShapeDtypeStruct((t, A_WIDTH), BF16),
            col_shape(N_PAIRS),
            jax.ShapeDtypeStruct((t, LANES), BF16),
            col_shape(B_HEADS),
            jax.ShapeDtypeStruct((t, B_HEADS * B_HEAD_PAD), BF16),
            col_shape(N_PAIRS),
        ],
        scratch_shapes=[pltpu.VMEM((A_HEADS, LANES), F32)],
        compiler_params=pltpu.CompilerParams(
            dimension_semantics=("arbitrary", "arbitrary"), vmem_limit_bytes=VMEM_LIMIT),
        name="inproj",
    )(x2, pos3, *weights)


def _attn_kernel(*refs, fox, nq):
    def tile(qi, carry):
        _attn_tile(qi, refs, fox, nq)
        return carry

    lax.fori_loop(0, nq, tile, 0)


def _attn_tile(qi, refs, fox, nq):
    if fox:
        qt_ref, k_ref, vt_ref, cb_ref, o_ref, m_scr, acc_scr, s_scr, smax_scr = refs
    else:
        qt_ref, k_ref, vt_ref, o_ref, m_scr, acc_scr, s_scr, smax_scr = refs
    head_dim = LANES // 2
    if fox:
        frow = lax.broadcasted_iota(jnp.int32, (LANES, TQ), 0)
        pair = pl.program_id(1)
        mine = [(frow < head_dim) == (h == 0) for h in range(2)]
        picks = [jnp.where((frow % A_HEADS == 2 * pair + h) & (frow < BIAS_PIECES * A_HEADS),
                           -1.0, 0.0).astype(BF16) for h in range(2)]

    def query_operands(tile):
        q0 = pl.multiple_of(tile * TQ, TQ)
        if fox:
            qt = qt_ref[:, pl.ds(q0, TQ)]
            return [jnp.concatenate([jnp.where(mine[h], qt, jnp.zeros_like(qt)), picks[h]], axis=0)
                    for h in range(2)]
        return [qt_ref[h, :, pl.ds(q0, TQ)] for h in range(2)]

    rhs = query_operands(qi)
    rhs_next = query_operands(jnp.minimum(qi + 1, nq - 1))

    m_scr[...] = jnp.full_like(m_scr, MASK_VALUE)
    acc_scr[...] = jnp.zeros_like(acc_scr)

    keys = lax.broadcasted_iota(jnp.int32, (TK, TQ), 0)
    queries = lax.broadcasted_iota(jnp.int32, (TK, TQ), 1)
    if fox:
        diag_mask = keys <= queries
    else:
        diag_mask = (keys // CHUNK) <= (queries // CHUNK)
    ones = jnp.ones((BF16_ROWS, TK), BF16)

    units = [(h, slice(c, c + QB)) for h in range(2) for c in range(0, TQ, QB)]

    def scores(j, h, cols, operands):
        start = pl.multiple_of(j * TK, TK)
        if fox:
            lhs = jnp.concatenate([k_ref[pl.ds(start, TK), :], cb_ref[pl.ds(start, TK), :]], axis=1)
        else:
            lhs = k_ref[pl.ds(start, TK), h * LANES:(h + 1) * LANES]
        return _dot(lhs, operands[h][:, cols])

    def put_scores(slot, j, h, cols, operands):
        s = scores(j, h, cols, operands)
        s_scr[slot, h, :, cols] = s
        smax_scr[slot, h, :, cols] = jnp.max(s, axis=0, keepdims=True)

    def update(j, slot, h, cols, masked):
        start = pl.multiple_of(j * TK, TK)
        nk = cols.stop if masked else TK
        s = s_scr[slot, h, :nk, cols]
        if masked:
            s = jnp.where(diag_mask[:nk, cols], s, MASK_VALUE)
            s_max = jnp.max(s, axis=0, keepdims=True)
        else:
            s_max = smax_scr[slot, h, :, cols]
        m_prev = m_scr[h, :, cols]
        m_new = jnp.maximum(m_prev, s_max)
        alpha = jnp.exp2(m_prev - m_new)
        p = jnp.exp2(s - m_new).astype(BF16)
        vt = jnp.concatenate(
            [vt_ref[h * head_dim:(h + 1) * head_dim, pl.ds(start, nk)], ones[:, :nk]], axis=0)
        acc_scr[h, :, cols] = alpha * acc_scr[h, :, cols] + _dot(vt, p)
        m_scr[h, :, cols] = m_new

    def stage(j, slot):
        nxt = (slot + 1) % SCORE_SLOTS
        for i in range(len(units) + 1):
            if i < len(units):
                h, cols = units[i]
                put_scores(nxt, j + 1, h, cols, rhs)
            if i > 0:
                update(j, slot, *units[i - 1], False)

    def last_stage(j, slot):
        for h, cols in units:
            if slot == 0:
                update(j, slot, h, cols, True)
            put_scores(0, 0, h, cols, rhs_next)
            if slot != 0:
                update(j, slot, h, cols, True)

    @pl.when(qi == 0)
    def _():
        for h, cols in units:
            put_scores(0, 0, h, cols, rhs)

    def body(i, carry):
        for u in range(ROUND_STAGES):
            stage(ROUND_STAGES * i + u, u % SCORE_SLOTS)
        return carry

    n_rounds = qi // ROUND_STAGES
    lax.fori_loop(0, n_rounds, body, 0)
    base = n_rounds * ROUND_STAGES
    for r in range(ROUND_STAGES):
        @pl.when(qi - base == r)
        def _(r=r):
            for u in range(r):
                stage(base + u, u % SCORE_SLOTS)
            last_stage(qi, r % SCORE_SLOTS)

    out_t = jnp.concatenate(
        [acc_scr[h, :head_dim] / acc_scr[h, head_dim:head_dim + 1] for h in range(2)], axis=0)
    o_ref[pl.ds(pl.multiple_of(qi * TQ, TQ), TQ), :] = out_t.T.astype(o_ref.dtype)


def _attention(qt, k, vt, cb, batch, seq, fox):
    t = k.shape[0]
    nq = seq // TQ
    assert TQ == TK
    qt_spec = pl.BlockSpec((None, None if fox else 2, LANES, seq), lambda b, p: (b, p, 0, 0))
    in_specs = [
        qt_spec,
        pl.BlockSpec((seq, LANES if fox else 2 * LANES), lambda b, p: (b, p)),
        pl.BlockSpec((None, None, LANES, seq), lambda b, p: (b, p, 0, 0)),
    ]
    args = [qt, k, vt]
    if fox:
        in_specs.append(pl.BlockSpec((seq, LANES), lambda b, p: (b, 0)))
        args.append(cb)
    acc_rows = LANES // 2 + BF16_ROWS
    return pl.pallas_call(
        functools.partial(_attn_kernel, fox=fox, nq=nq),
        grid=(batch, N_PAIRS),
        in_specs=in_specs,
        out_specs=pl.BlockSpec((seq, LANES), lambda b, p: (b, p)),
        out_shape=jax.ShapeDtypeStruct((t, A_WIDTH), BF16),
        scratch_shapes=[pltpu.VMEM((2, 1, TQ), F32), pltpu.VMEM((2, acc_rows, TQ), F32),
                        pltpu.VMEM((SCORE_SLOTS, 2, TK, TQ), F32),
                        pltpu.VMEM((SCORE_SLOTS, 2, 1, TQ), F32)],
        compiler_params=pltpu.CompilerParams(
            dimension_semantics=("arbitrary", "arbitrary"),
            vmem_limit_bytes=VMEM_LIMIT),
        name="fox_attention" if fox else "mla_attention",
    )(*args)


def _post_kernel(x_ref, ya_ref, yb_ref, g_ref, wga_ref, wgb_ref, bga_ref, bgb_ref, wa_ref,
                 wb_ref, wout_ref, gffn_ref, wgate_ref, wup_ref, wdown_ref, gfin_ref,
                 o_ref, act_scr):
    half = x_ref.shape[0] // 2
    ra, rb = slice(0, half), slice(half, 2 * half)

    def norm_in(rows):
        return _rms(x_ref[rows, :], g_ref[...]).astype(BF16)

    def gate(xn, w_ref, b_ref):
        return jax.nn.sigmoid(_dot(xn, w_ref[...]) + b_ref[...])

    def merge_out(rows, gate_a, gate_b):
        merged = (gate_a * _dot(ya_ref[rows, :], wa_ref[...])
                  + gate_b * _dot(yb_ref[rows, :], wb_ref[...]))
        return x_ref[rows, :] + _dot(merged.astype(BF16), wout_ref[...])

    def ffn_chunk(rows, hn, c):
        sl = slice(c, c + FFN_CHUNK)
        gt = _dot(hn, wgate_ref[:, sl])
        act_scr[rows, sl] = (gt * jax.nn.sigmoid(gt) * _dot(hn, wup_ref[:, sl])).astype(BF16)

    chunks = range(0, FFN_HIDDEN, FFN_CHUNK)

    xn_a = norm_in(ra)
    gate_aa = gate(xn_a, wga_ref, bga_ref)
    xn_b = norm_in(rb)
    h_a = merge_out(ra, gate_aa, gate(xn_a, wgb_ref, bgb_ref))
    gate_ab = gate(xn_b, wga_ref, bga_ref)
    hn_a = _rms(h_a, gffn_ref[...]).astype(BF16)
    h_b = merge_out(rb, gate_ab, gate(xn_b, wgb_ref, bgb_ref))
    ffn_chunk(ra, hn_a, chunks[0])
    hn_b = _rms(h_b, gffn_ref[...]).astype(BF16)
    for c in chunks[1:]:
        ffn_chunk(ra, hn_a, c)
    h_a = h_a + _dot(act_scr[ra, :], wdown_ref[...])
    ffn_chunk(rb, hn_b, chunks[0])
    o_ref[ra, :] = _rms(h_a, gfin_ref[...])
    for c in chunks[1:]:
        ffn_chunk(rb, hn_b, c)
    h_b = h_b + _dot(act_scr[rb, :], wdown_ref[...])
    o_ref[rb, :] = _rms(h_b, gfin_ref[...])


def _post(x2, ya, yb, *weights):
    t = x2.shape[0]
    row = lambda w: pl.BlockSpec((TM_OUT, w), lambda i: (i, 0))
    const = lambda w: pl.BlockSpec(w.shape, lambda i: (0,) * w.ndim,
                                   pipeline_mode=pl.Buffered(1))
    return pl.pallas_call(
        _post_kernel,
        grid=(t // TM_OUT,),
        in_specs=[row(D_MODEL), row(A_WIDTH), row(B_WIDTH)] + [const(w) for w in weights],
        out_specs=row(D_MODEL),
        out_shape=jax.ShapeDtypeStruct((t, D_MODEL), F32),
        scratch_shapes=[pltpu.VMEM((TM_OUT, FFN_HIDDEN), BF16)],
        compiler_params=pltpu.CompilerParams(
            dimension_semantics=("arbitrary",), vmem_limit_bytes=VMEM_LIMIT),
        name="merge_ffn",
    )(x2, ya, yb, *weights)


def _mla_lane_layout():
    half = LANES // 2
    nope = np.concatenate([np.arange(HALF_ROPE, half),
                           np.arange(half + HALF_ROPE, B_NOPE + 2 * HALF_ROPE)])
    rope = np.concatenate([np.arange(HALF_ROPE), half + np.arange(HALF_ROPE)])
    return nope, rope


def kernel(x, positions, norm_mix_g, w_in, b_forget, q_a_norm_g, w_q_up, kv_a_norm_g, w_kv_up,
           w_branch_a, w_branch_b, b_gate, w_out, norm_ffn_g, w_ffn_gate, w_ffn_up, w_ffn_down,
           norm_final_g):
    batch, seq, _ = x.shape
    t = batch * seq
    assert norm_mix_g.shape[0] == 1 and seq % TM_IN == 0 and seq % TQ == 0 and t % TM_OUT == 0
    x2 = x.reshape(t, D_MODEL)
    pos3 = positions.reshape(batch, 1, seq)
    nope_lane, rope_lane = _mla_lane_layout()

    w = w_in[0].astype(BF16)
    o = np.cumsum((0, A_WIDTH, A_WIDTH, A_WIDTH, A_HEADS, B_Q_RANK, B_KV_RANK, B_ROPE,
                   D_MODEL, D_MODEL))
    wq, wk, wv = (w[:, o[i]:o[i + 1]] for i in range(3))
    wf = jnp.zeros((D_MODEL, LANES), BF16).at[:, :A_HEADS].set(w[:, o[3]:o[4]])
    wcq, wckv = w[:, o[4]:o[5]], w[:, o[5]:o[6]]
    wkr = jnp.zeros((D_MODEL, LANES), BF16).at[:, rope_lane].set(w[:, o[6]:o[7]])
    wga, wgb = w[:, o[7]:o[8]], w[:, o[8]:o[9]]

    wqu = w_q_up[0].astype(BF16).reshape(B_Q_RANK, B_HEADS, B_NOPE + B_ROPE)
    wqup = jnp.zeros((B_Q_RANK, B_HEADS, B_HEAD_PAD), BF16)
    wqup = wqup.at[:, :, nope_lane].set(wqu[:, :, :B_NOPE]).at[:, :, rope_lane].set(wqu[:, :, B_NOPE:])
    wqup = wqup.reshape(B_Q_RANK, B_HEADS * B_HEAD_PAD)
    wkvu = w_kv_up[0].astype(BF16).reshape(B_KV_RANK, B_HEADS, B_NOPE + B_V_DIM)
    wkup = jnp.zeros((B_KV_RANK, B_HEADS, B_HEAD_PAD), BF16).at[:, :, nope_lane].set(wkvu[:, :, :B_NOPE])
    wkup = wkup.reshape(B_KV_RANK, B_HEADS * B_HEAD_PAD)
    wvup = wkvu[:, :, B_NOPE:].reshape(B_KV_RANK, B_WIDTH)

    invf = (ROPE_BASE ** (-jnp.arange(HALF_ROPE, dtype=F32) / HALF_ROPE)).reshape(HALF_ROPE, 1)
    bf = jnp.broadcast_to(b_forget[0].astype(F32)[:, None], (A_HEADS, TM_IN))
    row = lambda v: v.astype(F32).reshape(1, -1)

    qat, ka, vat, cb, qbt, kb, vbt = _inproj(
        x2, pos3, row(norm_mix_g[0]), wq, wk, wv, wf, wcq, wckv, wkr, bf, invf,
        row(q_a_norm_g[0]), wqup, row(kv_a_norm_g[0]), wkup, wvup, batch, seq)

    ya = _attention(qat, ka, vat, cb, batch, seq, fox=True)
    yb = _attention(qbt, kb, vbt, None, batch, seq, fox=False)

    out = _post(
        x2, ya, yb, row(norm_mix_g[0]), wga, wgb, row(b_gate[0, :D_MODEL]), row(b_gate[0, D_MODEL:]),
        w_branch_a[0].astype(BF16), w_branch_b[0].astype(BF16), w_out[0].astype(BF16),
        row(norm_ffn_g[0]), w_ffn_gate[0].astype(BF16), w_ffn_up[0].astype(BF16),
        w_ffn_down[0].astype(BF16), row(norm_final_g))
    return out.reshape(batch, seq, D_MODEL)
```

```python
import functools

import jax
import jax.numpy as jnp
import numpy as np
from jax import lax
from jax.experimental import pallas as pl
from jax.experimental.pallas import tpu as pltpu

D_MODEL = 1024
CHUNK = 64
A_HEADS = 8
A_HEAD_DIM = 64
A_WIDTH = A_HEADS * A_HEAD_DIM
B_HEADS = 8
B_Q_RANK = 256
B_KV_RANK = 128
B_NOPE = 64
B_ROPE = 32
B_V_DIM = 64
B_WIDTH = B_HEADS * B_V_DIM
ROPE_BASE = 10000.0
FFN_HIDDEN = 2816
NORM_EPS = 1e-6
MASK_VALUE = -1e30

LANES = 128
BF16_ROWS = 16
LOG2E = 1.4426950408889634
A_QSCALE = (A_HEAD_DIM ** -0.5) * LOG2E
B_QSCALE = ((B_NOPE + B_ROPE) ** -0.5) * LOG2E
HALF_ROPE = B_ROPE // 2
B_HEAD_PAD = LANES
N_PAIRS = A_HEADS // 2
BIAS_PIECES = 3

TM_IN = 512
TM_OUT = 512
TQ = 512
TK = 512
SCORE_SLOTS = 3
ROUND_STAGES = 6
QB = 256
FFN_CHUNK = 256
V7X_VMEM_BYTES = 64 * 1024 * 1024
VMEM_LIMIT = V7X_VMEM_BYTES * 7 // 8

F32 = jnp.float32
BF16 = jnp.bfloat16


def _rms(x, g):
    ms = jnp.mean(x * x, axis=-1, keepdims=True)
    return x * lax.rsqrt(ms + NORM_EPS) * g


def _dot(a, b):
    return jnp.dot(a, b, preferred_element_type=F32)


def _rope(t, cos, sin_signed):
    return t * cos + pltpu.roll(t, LANES // 2, axis=1) * sin_signed


def _inproj_kernel(x_ref, pos_ref, g_ref, wq_ref, wk_ref, wv_ref, wf_ref, wcq_ref,
                   wckv_ref, wkr_ref, bf_ref, invf_ref, gq_ref, wqup_ref,
                   gkv_ref, wkup_ref, wvup_ref,
                   qat_ref, ka_ref, vat_ref, cb_ref, qbt_ref, kb_ref, vbt_ref, carry_ref):
    @pl.when(pl.program_id(1) == 0)
    def _():
        carry_ref[...] = jnp.zeros_like(carry_ref)

    tm = x_ref.shape[0]
    xn = _rms(x_ref[...], g_ref[...]).astype(BF16)

    def store_transposed(ref, val, groups):
        for i in range(groups):
            ref[i] = val[:, i * LANES:(i + 1) * LANES].T.astype(BF16)

    f = _dot(xn, wf_ref[...])
    cq = _dot(xn, wcq_ref[...])
    ckv = _dot(xn, wckv_ref[...])
    kr = _dot(xn, wkr_ref[...])
    cqn = _rms(cq, gq_ref[...]).astype(BF16)
    ckvn = _rms(ckv, gkv_ref[...]).astype(BF16)
    qb = _dot(cqn, wqup_ref[...])
    q = _dot(xn, wq_ref[...])

    z = f.T[:A_HEADS] + bf_ref[...]
    c = jnp.minimum(z, 0.0) - jnp.log1p(jnp.exp(-jnp.abs(z)))
    lane = lax.broadcasted_iota(jnp.int32, c.shape, 1)
    d = 1
    while d < tm:
        c = c + jnp.where(lane >= d, pltpu.roll(c, d, axis=1), 0.0)
        d *= 2
    c = c + jnp.tile(carry_ref[...], (1, tm // LANES))
    carry_ref[...] = jnp.broadcast_to(c[:, tm - 1:tm], carry_ref.shape)
    c = c * LOG2E
    hi = c.astype(BF16).astype(F32)
    mid = (c - hi).astype(BF16).astype(F32)
    lo = c - hi - mid
    pad = jnp.zeros((LANES - BIAS_PIECES * A_HEADS, tm), F32)
    cb_ref[...] = jnp.concatenate([hi, mid, lo, pad], axis=0).T.astype(BF16)

    ang = invf_ref[...] * pos_ref[...].astype(F32)
    cos_h, sin_h = jnp.cos(ang), jnp.sin(ang)
    fill = (LANES // 2 - HALF_ROPE, tm)
    cos = jnp.concatenate([cos_h, jnp.ones(fill, F32)] * 2, axis=0).T
    sin_signed = jnp.concatenate(
        [-sin_h, jnp.zeros(fill, F32), sin_h, jnp.zeros(fill, F32)], axis=0).T

    k = _dot(xn, wk_ref[...])
    for h in range(B_HEADS):
        sl = slice(h * B_HEAD_PAD, (h + 1) * B_HEAD_PAD)
        qbt_ref[h] = (_rope(qb[:, sl], cos, sin_signed) * B_QSCALE).T.astype(BF16)
    ka_ref[...] = k.astype(BF16)

    v = _dot(xn, wv_ref[...])
    store_transposed(qat_ref, q * A_QSCALE, N_PAIRS)
    kr = _rope(kr, cos, sin_signed)

    kn = _dot(ckvn, wkup_ref[...])
    store_transposed(vat_ref, v, N_PAIRS)
    vb = _dot(ckvn, wvup_ref[...])
    for h in range(B_HEADS):
        sl = slice(h * B_HEAD_PAD, (h + 1) * B_HEAD_PAD)
        kb_ref[:, sl] = (kn[:, sl] + kr).astype(BF16)
    store_transposed(vbt_ref, vb, N_PAIRS)


def _full(shape):
    return pl.BlockSpec(shape, lambda *_: (0,) * len(shape))


def _inproj(x2, pos3, g, wq, wk, wv, wf, wcq, wckv, wkr, bf, invf, gq, wqup, gkv,
            wkup, wvup, batch, seq):
    t = x2.shape[0]
    nt = seq // TM_IN
    row = lambda w: pl.BlockSpec((TM_IN, w), lambda b, s: (b * nt + s, 0))
    col = lambda n: pl.BlockSpec((None, n, LANES, TM_IN), lambda b, s: (b, 0, 0, s))
    col_shape = lambda n: jax.ShapeDtypeStruct((batch, n, LANES, seq), BF16)
    pos_spec = pl.BlockSpec((None, 1, TM_IN), lambda b, s: (b, 0, s))
    weights = (g, wq, wk, wv, wf, wcq, wckv, wkr, bf, invf, gq, wqup, gkv, wkup, wvup)
    return pl.pallas_call(
        _inproj_kernel,
        grid=(batch, nt),
        in_specs=[row(D_MODEL), pos_spec] + [_full(w.shape) for w in weights],
        out_specs=[
            col(N_PAIRS), row(A_WIDTH), col(N_PAIRS), row(LANES),
            col(B_HEADS), row(B_HEADS * B_HEAD_PAD), col(N_PAIRS),
        ],
        out_shape=[
            col_shape(N_PAIRS),
            jax.ShapeDtypeStruct((t, A_WIDTH), BF16),
            col_shape(N_PAIRS),
            jax.ShapeDtypeStruct((t, LANES), BF16),
            col_shape(B_HEADS),
            jax.ShapeDtypeStruct((t, B_HEADS * B_HEAD_PAD), BF16),
            col_shape(N_PAIRS),
        ],
        scratch_shapes=[pltpu.VMEM((A_HEADS, LANES), F32)],
        compiler_params=pltpu.CompilerParams(
            dimension_semantics=("arbitrary", "arbitrary"), vmem_limit_bytes=VMEM_LIMIT),
        name="inproj",
    )(x2, pos3, *weights)


def _attn_kernel(*refs, fox, nq):
    def tile(qi, carry):
        _attn_tile(qi, refs, fox, nq)
        return carry

    lax.fori_loop(0, nq, tile, 0)


def _attn_tile(qi, refs, fox, nq):
    if fox:
        qt_ref, k_ref, vt_ref, cb_ref, o_ref, m_scr, acc_scr, s_scr, smax_scr = refs
    else:
        qt_ref, k_ref, vt_ref, o_ref, m_scr, acc_scr, s_scr, smax_scr = refs
    head_dim = LANES // 2
    if fox:
        frow = lax.broadcasted_iota(jnp.int32, (LANES, TQ), 0)
        pair = pl.program_id(1)
        mine = [(frow < head_dim) == (h == 0) for h in range(2)]
        picks = [jnp.where((frow % A_HEADS == 2 * pair + h) & (frow < BIAS_PIECES * A_HEADS),
                           -1.0, 0.0).astype(BF16) for h in range(2)]

    def query_operands(tile):
        q0 = pl.multiple_of(tile * TQ, TQ)
        if fox:
            qt = qt_ref[:, pl.ds(q0, TQ)]
            return [jnp.concatenate([jnp.where(mine[h], qt, jnp.zeros_like(qt)), picks[h]], axis=0)
                    for h in range(2)]
        return [qt_ref[h, :, pl.ds(q0, TQ)] for h in range(2)]

    rhs = query_operands(qi)
    rhs_next = query_operands(jnp.minimum(qi + 1, nq - 1))

    m_scr[...] = jnp.full_like(m_scr, MASK_VALUE)
    acc_scr[...] = jnp.zeros_like(acc_scr)

    keys = lax.broadcasted_iota(jnp.int32, (TK, TQ), 0)
    queries = lax.broadcasted_iota(jnp.int32, (TK, TQ), 1)
    if fox:
        diag_mask = keys <= queries
    else:
        diag_mask = (keys // CHUNK) <= (queries // CHUNK)
    ones = jnp.ones((BF16_ROWS, TK), BF16)

    units = [(h, slice(c, c + QB)) for h in range(2) for c in range(0, TQ, QB)]

    def scores(j, h, cols, operands):
        start = pl.multiple_of(j * TK, TK)
        if fox:
            lhs = jnp.concatenate([k_ref[pl.ds(start, TK), :], cb_ref[pl.ds(start, TK), :]], axis=1)
        else:
            lhs = k_ref[pl.ds(start, TK), h * LANES:(h + 1) * LANES]
        return _dot(lhs, operands[h][:, cols])

    def put_scores(slot, j, h, cols, operands):
        s = scores(j, h, cols, operands)
        s_scr[slot, h, :, cols] = s
        smax_scr[slot, h, :, cols] = jnp.max(s, axis=0, keepdims=True)

    def update(j, slot, h, cols, masked):
        start = pl.multiple_of(j * TK, TK)
        nk = cols.stop if masked else TK
        s = s_scr[slot, h, :nk, cols]
        if masked:
            s = jnp.where(diag_mask[:nk, cols], s, MASK_VALUE)
            s_max = jnp.max(s, axis=0, keepdims=True)
        else:
            s_max = smax_scr[slot, h, :, cols]
        m_prev = m_scr[h, :, cols]
        m_new = jnp.maximum(m_prev, s_max)
        alpha = jnp.exp2(m_prev - m_new)
        p = jnp.exp2(s - m_new).astype(BF16)
        vt = jnp.concatenate(
            [vt_ref[h * head_dim:(h + 1) * head_dim, pl.ds(start, nk)], ones[:, :nk]], axis=0)
        acc_scr[h, :, cols] = alpha * acc_scr[h, :, cols] + _dot(vt, p)
        m_scr[h, :, cols] = m_new

    def stage(j, slot):
        nxt = (slot + 1) % SCORE_SLOTS
        for i in range(len(units) + 1):
            if i < len(units):
                h, cols = units[i]
                put_scores(nxt, j + 1, h, cols, rhs)
            if i > 0:
                update(j, slot, *units[i - 1], False)

    def last_stage(j, slot):
        for h, cols in units:
            if slot == 0:
                update(j, slot, h, cols, True)
            put_scores(0, 0, h, cols, rhs_next)
            if slot != 0:
                update(j, slot, h, cols, True)

    @pl.when(qi == 0)
    def _():
        for h, cols in units:
            put_scores(0, 0, h, cols, rhs)

    def body(i, carry):
        for u in range(ROUND_STAGES):
            stage(ROUND_STAGES * i + u, u % SCORE_SLOTS)
        return carry

    n_rounds = qi // ROUND_STAGES
    lax.fori_loop(0, n_rounds, body, 0)
    base = n_rounds * ROUND_STAGES
    for r in range(ROUND_STAGES):
        @pl.when(qi - base == r)
        def _(r=r):
            for u in range(r):
                stage(base + u, u % SCORE_SLOTS)
            last_stage(qi, r % SCORE_SLOTS)

    out_t = jnp.concatenate(
        [acc_scr[h, :head_dim] / acc_scr[h, head_dim:head_dim + 1] for h in range(2)], axis=0)
    o_ref[pl.ds(pl.multiple_of(qi * TQ, TQ), TQ), :] = out_t.T.astype(o_ref.dtype)


def _attention(qt, k, vt, cb, batch, seq, fox):
    t = k.shape[0]
    nq = seq // TQ
    assert TQ == TK and TQ % QB == 0 and QB % CHUNK == 0 and ROUND_STAGES % SCORE_SLOTS == 0
    qt_spec = pl.BlockSpec((None, None if fox else 2, LANES, seq), lambda b, p: (b, p, 0, 0))
    in_specs = [
        qt_spec,
        pl.BlockSpec((seq, LANES if fox else 2 * LANES), lambda b, p: (b, p)),
        pl.BlockSpec((None, None, LANES, seq), lambda b, p: (b, p, 0, 0)),
    ]
    args = [qt, k, vt]
    if fox:
        in_specs.append(pl.BlockSpec((seq, LANES), lambda b, p: (b, 0)))
        args.append(cb)
    acc_rows = LANES // 2 + BF16_ROWS
    return pl.pallas_call(
        functools.partial(_attn_kernel, fox=fox, nq=nq),
        grid=(batch, N_PAIRS),
        in_specs=in_specs,
        out_specs=pl.BlockSpec((seq, LANES), lambda b, p: (b, p)),
        out_shape=jax.ShapeDtypeStruct((t, A_WIDTH), BF16),
        scratch_shapes=[pltpu.VMEM((2, 1, TQ), F32), pltpu.VMEM((2, acc_rows, TQ), F32),
                        pltpu.VMEM((SCORE_SLOTS, 2, TK, TQ), F32),
                        pltpu.VMEM((SCORE_SLOTS, 2, 1, TQ), F32)],
        compiler_params=pltpu.CompilerParams(
            dimension_semantics=("arbitrary", "arbitrary"),
            vmem_limit_bytes=VMEM_LIMIT),
        name="fox_attention" if fox else "mla_attention",
    )(*args)


def _post_kernel(x_ref, ya_ref, yb_ref, g_ref, wga_ref, wgb_ref, bga_ref, bgb_ref, wa_ref,
                 wb_ref, wout_ref, gffn_ref, wgate_ref, wup_ref, wdown_ref, gfin_ref,
                 o_ref, act_scr):
    half = x_ref.shape[0] // 2
    ra, rb = slice(0, half), slice(half, 2 * half)

    def norm_in(rows):
        return _rms(x_ref[rows, :], g_ref[...]).astype(BF16)

    def gate(xn, w_ref, b_ref):
        return jax.nn.sigmoid(_dot(xn, w_ref[...]) + b_ref[...])

    def merge_out(rows, gate_a, gate_b):
        merged = (gate_a * _dot(ya_ref[rows, :], wa_ref[...])
                  + gate_b * _dot(yb_ref[rows, :], wb_ref[...]))
        return x_ref[rows, :] + _dot(merged.astype(BF16), wout_ref[...])

    def ffn_chunk(rows, hn, c):
        sl = slice(c, c + FFN_CHUNK)
        gt = _dot(hn, wgate_ref[:, sl])
        act_scr[rows, sl] = (gt * jax.nn.sigmoid(gt) * _dot(hn, wup_ref[:, sl])).astype(BF16)

    chunks = range(0, FFN_HIDDEN, FFN_CHUNK)

    xn_a = norm_in(ra)
    gate_aa = gate(xn_a, wga_ref, bga_ref)
    xn_b = norm_in(rb)
    h_a = merge_out(ra, gate_aa, gate(xn_a, wgb_ref, bgb_ref))
    gate_ab = gate(xn_b, wga_ref, bga_ref)
    hn_a = _rms(h_a, gffn_ref[...]).astype(BF16)
    h_b = merge_out(rb, gate_ab, gate(xn_b, wgb_ref, bgb_ref))
    ffn_chunk(ra, hn_a, chunks[0])
    hn_b = _rms(h_b, gffn_ref[...]).astype(BF16)
    for c in chunks[1:]:
        ffn_chunk(ra, hn_a, c)
    h_a = h_a + _dot(act_scr[ra, :], wdown_ref[...])
    ffn_chunk(rb, hn_b, chunks[0])
    o_ref[ra, :] = _rms(h_a, gfin_ref[...])
    for c in chunks[1:]:
        ffn_chunk(rb, hn_b, c)
    h_b = h_b + _dot(act_scr[rb, :], wdown_ref[...])
    o_ref[rb, :] = _rms(h_b, gfin_ref[...])


def _post(x2, ya, yb, *weights):
    t = x2.shape[0]
    row = lambda w: pl.BlockSpec((TM_OUT, w), lambda i: (i, 0))
    const = lambda w: pl.BlockSpec(w.shape, lambda i: (0,) * w.ndim,
                                   pipeline_mode=pl.Buffered(1))
    return pl.pallas_call(
        _post_kernel,
        grid=(t // TM_OUT,),
        in_specs=[row(D_MODEL), row(A_WIDTH), row(B_WIDTH)] + [const(w) for w in weights],
        out_specs=row(D_MODEL),
        out_shape=jax.ShapeDtypeStruct((t, D_MODEL), F32),
        scratch_shapes=[pltpu.VMEM((TM_OUT, FFN_HIDDEN), BF16)],
        compiler_params=pltpu.CompilerParams(
            dimension_semantics=("arbitrary",), vmem_limit_bytes=VMEM_LIMIT),
        name="merge_ffn",
    )(x2, ya, yb, *weights)


def _mla_lane_layout():
    half = LANES // 2
    nope = np.concatenate([np.arange(HALF_ROPE, half),
                           np.arange(half + HALF_ROPE, B_NOPE + 2 * HALF_ROPE)])
    rope = np.concatenate([np.arange(HALF_ROPE), half + np.arange(HALF_ROPE)])
    return nope, rope


def kernel(x, positions, norm_mix_g, w_in, b_forget, q_a_norm_g, w_q_up, kv_a_norm_g, w_kv_up,
           w_branch_a, w_branch_b, b_gate, w_out, norm_ffn_g, w_ffn_gate, w_ffn_up, w_ffn_down,
           norm_final_g):
    batch, seq, _ = x.shape
    t = batch * seq
    assert norm_mix_g.shape[0] == 1 and seq % TM_IN == 0 and seq % TQ == 0 and t % TM_OUT == 0
    x2 = x.reshape(t, D_MODEL)
    pos3 = positions.reshape(batch, 1, seq)
    nope_lane, rope_lane = _mla_lane_layout()

    w = w_in[0].astype(BF16)
    o = np.cumsum((0, A_WIDTH, A_WIDTH, A_WIDTH, A_HEADS, B_Q_RANK, B_KV_RANK, B_ROPE,
                   D_MODEL, D_MODEL))
    wq, wk, wv = (w[:, o[i]:o[i + 1]] for i in range(3))
    wf = jnp.zeros((D_MODEL, LANES), BF16).at[:, :A_HEADS].set(w[:, o[3]:o[4]])
    wcq, wckv = w[:, o[4]:o[5]], w[:, o[5]:o[6]]
    wkr = jnp.zeros((D_MODEL, LANES), BF16).at[:, rope_lane].set(w[:, o[6]:o[7]])
    wga, wgb = w[:, o[7]:o[8]], w[:, o[8]:o[9]]

    wqu = w_q_up[0].astype(BF16).reshape(B_Q_RANK, B_HEADS, B_NOPE + B_ROPE)
    wqup = jnp.zeros((B_Q_RANK, B_HEADS, B_HEAD_PAD), BF16)
    wqup = wqup.at[:, :, nope_lane].set(wqu[:, :, :B_NOPE]).at[:, :, rope_lane].set(wqu[:, :, B_NOPE:])
    wqup = wqup.reshape(B_Q_RANK, B_HEADS * B_HEAD_PAD)
    wkvu = w_kv_up[0].astype(BF16).reshape(B_KV_RANK, B_HEADS, B_NOPE + B_V_DIM)
    wkup = jnp.zeros((B_KV_RANK, B_HEADS, B_HEAD_PAD), BF16).at[:, :, nope_lane].set(wkvu[:, :, :B_NOPE])
    wkup = wkup.reshape(B_KV_RANK, B_HEADS * B_HEAD_PAD)
    wvup = wkvu[:, :, B_NOPE:].reshape(B_KV_RANK, B_WIDTH)

    invf = (ROPE_BASE ** (-jnp.arange(HALF_ROPE, dtype=F32) / HALF_ROPE)).reshape(HALF_ROPE, 1)
    bf = jnp.broadcast_to(b_forget[0].astype(F32)[:, None], (A_HEADS, TM_IN))
    row = lambda v: v.astype(F32).reshape(1, -1)

    qat, ka, vat, cb, qbt, kb, vbt = _inproj(
        x2, pos3, row(norm_mix_g[0]), wq, wk, wv, wf, wcq, wckv, wkr, bf, invf,
        row(q_a_norm_g[0]), wqup, row(kv_a_norm_g[0]), wkup, wvup, batch, seq)

    ya = _attention(qat, ka, vat, cb, batch, seq, fox=True)
    yb = _attention(qbt, kb, vbt, None, batch, seq, fox=False)

    out = _post(
        x2, ya, yb, row(norm_mix_g[0]), wga, wgb, row(b_gate[0, :D_MODEL]), row(b_gate[0, D_MODEL:]),
        w_branch_a[0].astype(BF16), w_branch_b[0].astype(BF16), w_out[0].astype(BF16),
        row(norm_ffn_g[0]), w_ffn_gate[0].astype(BF16), w_ffn_up[0].astype(BF16),
        w_ffn_down[0].astype(BF16), row(norm_final_g))
    return out.reshape(batch, seq, D_MODEL)
```

```python
import functools

import jax
import jax.numpy as jnp
import numpy as np
from jax import lax
from jax.experimental import pallas as pl
from jax.experimental.pallas import tpu as pltpu

D_MODEL = 1024
CHUNK = 64
A_HEADS = 8
A_HEAD_DIM = 64
A_WIDTH = A_HEADS * A_HEAD_DIM
B_HEADS = 8
B_Q_RANK = 256
B_KV_RANK = 128
B_NOPE = 64
B_ROPE = 32
B_V_DIM = 64
B_WIDTH = B_HEADS * B_V_DIM
ROPE_BASE = 10000.0
FFN_HIDDEN = 2816
NORM_EPS = 1e-6
MASK_VALUE = -1e30

LANES = 128
BF16_ROWS = 16
LOG2E = 1.4426950408889634
A_QSCALE = (A_HEAD_DIM ** -0.5) * LOG2E
B_QSCALE = ((B_NOPE + B_ROPE) ** -0.5) * LOG2E
HALF_ROPE = B_ROPE // 2
B_HEAD_PAD = LANES
N_PAIRS = A_HEADS // 2
BIAS_PIECES = 3

TM_IN = 512
TM_OUT = 512
TQ = 512
TK = 512
SCORE_SLOTS = 3
ROUND_STAGES = 6
QB = 256
FFN_CHUNK = 256
V7X_VMEM_BYTES = 64 * 1024 * 1024
VMEM_LIMIT = V7X_VMEM_BYTES * 7 // 8

F32 = jnp.float32
BF16 = jnp.bfloat16


def _rms(x, g):
    ms = jnp.mean(x * x, axis=-1, keepdims=True)
    return x * lax.rsqrt(ms + NORM_EPS) * g


def _dot(a, b):
    return jnp.dot(a, b, preferred_element_type=F32)


def _rope(t, cos, sin_signed):
    return t * cos + pltpu.roll(t, LANES // 2, axis=1) * sin_signed


def _inproj_kernel(x_ref, pos_ref, g_ref, wq_ref, wk_ref, wv_ref, wf_ref, wcq_ref,
                   wckv_ref, wkr_ref, bf_ref, invf_ref, gq_ref, wqup_ref,
                   gkv_ref, wkup_ref, wvup_ref,
                   qat_ref, ka_ref, vat_ref, cb_ref, qbt_ref, kb_ref, vbt_ref, carry_ref):
    @pl.when(pl.program_id(1) == 0)
    def _():
        carry_ref[...] = jnp.zeros_like(carry_ref)

    tm = x_ref.shape[0]
    xn = _rms(x_ref[...], g_ref[...]).astype(BF16)

    def store_transposed(ref, val, groups):
        for i in range(groups):
            ref[i] = val[:, i * LANES:(i + 1) * LANES].T.astype(BF16)

    f = _dot(xn, wf_ref[...])
    cq = _dot(xn, wcq_ref[...])
    ckv = _dot(xn, wckv_ref[...])
    kr = _dot(xn, wkr_ref[...])
    cqn = _rms(cq, gq_ref[...]).astype(BF16)
    ckvn = _rms(ckv, gkv_ref[...]).astype(BF16)
    qb = _dot(cqn, wqup_ref[...])
    q = _dot(xn, wq_ref[...])

    z = f.T[:A_HEADS] + bf_ref[...]
    c = jnp.minimum(z, 0.0) - jnp.log1p(jnp.exp(-jnp.abs(z)))
    lane = lax.broadcasted_iota(jnp.int32, c.shape, 1)
    d = 1
    while d < tm:
        c = c + jnp.where(lane >= d, pltpu.roll(c, d, axis=1), 0.0)
        d *= 2
    c = c + jnp.tile(carry_ref[...], (1, tm // LANES))
    carry_ref[...] = jnp.broadcast_to(c[:, tm - 1:tm], carry_ref.shape)
    c = c * LOG2E
    hi = c.astype(BF16).astype(F32)
    mid = (c - hi).astype(BF16).astype(F32)
    lo = c - hi - mid
    pad = jnp.zeros((LANES - BIAS_PIECES * A_HEADS, tm), F32)
    cb_ref[...] = jnp.concatenate([hi, mid, lo, pad], axis=0).T.astype(BF16)

    ang = invf_ref[...] * pos_ref[...].astype(F32)
    cos_h, sin_h = jnp.cos(ang), jnp.sin(ang)
    fill = (LANES // 2 - HALF_ROPE, tm)
    cos = jnp.concatenate([cos_h, jnp.ones(fill, F32)] * 2, axis=0).T
    sin_signed = jnp.concatenate(
        [-sin_h, jnp.zeros(fill, F32), sin_h, jnp.zeros(fill, F32)], axis=0).T

    k = _dot(xn, wk_ref[...])
    for h in range(B_HEADS):
        sl = slice(h * B_HEAD_PAD, (h + 1) * B_HEAD_PAD)
        qbt_ref[h] = (_rope(qb[:, sl], cos, sin_signed) * B_QSCALE).T.astype(BF16)
    ka_ref[...] = k.astype(BF16)

    v = _dot(xn, wv_ref[...])
    store_transposed(qat_ref, q * A_QSCALE, N_PAIRS)
    kr = _rope(kr, cos, sin_signed)

    kn = _dot(ckvn, wkup_ref[...])
    store_transposed(vat_ref, v, N_PAIRS)
    vb = _dot(ckvn, wvup_ref[...])
    for h in range(B_HEADS):
        sl = slice(h * B_HEAD_PAD, (h + 1) * B_HEAD_PAD)
        kb_ref[:, sl] = (kn[:, sl] + kr).astype(BF16)
    store_transposed(vbt_ref, vb, N_PAIRS)


def _full(shape):
    return pl.BlockSpec(shape, lambda *_: (0,) * len(shape))


def _inproj(x2, pos3, g, wq, wk, wv, wf, wcq, wckv, wkr, bf, invf, gq, wqup, gkv,
            wkup, wvup, batch, seq):
    t = x2.shape[0]
    nt = seq // TM_IN
    row = lambda w: pl.BlockSpec((TM_IN, w), lambda b, s: (b * nt + s, 0))
    col = lambda n: pl.BlockSpec((None, n, LANES, TM_IN), lambda b, s: (b, 0, 0, s))
    col_shape = lambda n: jax.ShapeDtypeStruct((batch, n, LANES, seq), BF16)
    pos_spec = pl.BlockSpec((None, 1, TM_IN), lambda b, s: (b, 0, s))
    weights = (g, wq, wk, wv, wf, wcq, wckv, wkr, bf, invf, gq, wqup, gkv, wkup, wvup)
    return pl.pallas_call(
        _inproj_kernel,
        grid=(batch, nt),
        in_specs=[row(D_MODEL), pos_spec] + [_full(w.shape) for w in weights],
        out_specs=[
            col(N_PAIRS), row(A_WIDTH), col(N_PAIRS), row(LANES),
            col(B_HEADS), row(B_HEADS * B_HEAD_PAD), col(N_PAIRS),
        ],
        out_shape=[
            col_shape(N_PAIRS),
            jax.ShapeDtypeStruct((t, A_WIDTH), BF16),
            col_shape(N_PAIRS),
            jax.ShapeDtypeStruct((t, LANES), BF16),
            col_shape(B_HEADS),
            jax.ShapeDtypeStruct((t, B_HEADS * B_HEAD_PAD), BF16),
            col_shape(N_PAIRS),
        ],
        scratch_shapes=[pltpu.VMEM((A_HEADS, LANES), F32)],
        compiler_params=pltpu.CompilerParams(
            dimension_semantics=("arbitrary", "arbitrary"), vmem_limit_bytes=VMEM_LIMIT),
        name="inproj",
    )(x2, pos3, *weights)


def _attn_kernel(*refs, fox, nq):
    def tile(qi, carry):
        _attn_tile(qi, refs, fox, nq)
        return carry

    lax.fori_loop(0, nq, tile, 0)


def _attn_tile(qi, refs, fox, nq):
    if fox:
        qt_ref, k_ref, vt_ref, cb_ref, o_ref, m_scr, acc_scr, s_scr, smax_scr = refs
    else:
        qt_ref, k_ref, vt_ref, o_ref, m_scr, acc_scr, s_scr, smax_scr = refs
    head_dim = LANES // 2
    if fox:
        frow = lax.broadcasted_iota(jnp.int32, (LANES, TQ), 0)
        pair = pl.program_id(1)
        mine = [(frow < head_dim) == (h == 0) for h in range(2)]
        picks = [jnp.where((frow % A_HEADS == 2 * pair + h) & (frow < BIAS_PIECES * A_HEADS),
                           -1.0, 0.0).astype(BF16) for h in range(2)]

    def query_operands(tile):
        q0 = pl.multiple_of(tile * TQ, TQ)
        if fox:
            qt = qt_ref[:, pl.ds(q0, TQ)]
            return [jnp.concatenate([jnp.where(mine[h], qt, jnp.zeros_like(qt)), picks[h]], axis=0)
                    for h in range(2)]
        return [qt_ref[h, :, pl.ds(q0, TQ)] for h in range(2)]

    rhs = query_operands(qi)
    rhs_next = query_operands(jnp.minimum(qi + 1, nq - 1))

    m_scr[...] = jnp.full_like(m_scr, MASK_VALUE)
    acc_scr[...] = jnp.zeros_like(acc_scr)

    keys = lax.broadcasted_iota(jnp.int32, (TK, TQ), 0)
    queries = lax.broadcasted_iota(jnp.int32, (TK, TQ), 1)
    if fox:
        diag_mask = keys <= queries
    else:
        diag_mask = (keys // CHUNK) <= (queries // CHUNK)
    ones = jnp.ones((BF16_ROWS, TK), BF16)

    units = [(h, slice(c, c + QB)) for h in range(2) for c in range(0, TQ, QB)]

    def scores(j, h, cols, operands):
        start = pl.multiple_of(j * TK, TK)
        if fox:
            lhs = jnp.concatenate([k_ref[pl.ds(start, TK), :], cb_ref[pl.ds(start, TK), :]], axis=1)
        else:
            lhs = k_ref[pl.ds(start, TK), h * LANES:(h + 1) * LANES]
        return _dot(lhs, operands[h][:, cols])

    def put_scores(slot, j, h, cols, operands):
        s = scores(j, h, cols, operands)
        s_scr[slot, h, :, cols] = s
        smax_scr[slot, h, :, cols] = jnp.max(s, axis=0, keepdims=True)

    def update(j, slot, h, cols, masked):
        start = pl.multiple_of(j * TK, TK)
        nk = cols.stop if masked else TK
        s = s_scr[slot, h, :nk, cols]
        if masked:
            s = jnp.where(diag_mask[:nk, cols], s, MASK_VALUE)
            s_max = jnp.max(s, axis=0, keepdims=True)
        else:
            s_max = smax_scr[slot, h, :, cols]
        m_prev = m_scr[h, :, cols]
        m_new = jnp.maximum(m_prev, s_max)
        alpha = jnp.exp2(m_prev - m_new)
        p = jnp.exp2(s - m_new).astype(BF16)
        vt = jnp.concatenate(
            [vt_ref[h * head_dim:(h + 1) * head_dim, pl.ds(start, nk)], ones[:, :nk]], axis=0)
        acc_scr[h, :, cols] = alpha * acc_scr[h, :, cols] + _dot(vt, p)
        m_scr[h, :, cols] = m_new

    def stage(j, slot):
        nxt = (slot + 1) % SCORE_SLOTS
        for i in range(len(units) + 1):
            if i < len(units):
                h, cols = units[i]
                put_scores(nxt, j + 1, h, cols, rhs)
            if i > 0:
                update(j, slot, *units[i - 1], False)

    def last_stage(j, slot):
        for h, cols in units:
            final = (h, cols) == units[-1]
            if slot == 0 or final:
                update(j, slot, h, cols, True)
            if final:
                finish()
            put_scores(0, 0, h, cols, rhs_next)
            if slot != 0 and not final:
                update(j, slot, h, cols, True)

    def finish():
        out_t = jnp.concatenate(
            [acc_scr[h, :head_dim] / acc_scr[h, head_dim:head_dim + 1] for h in range(2)], axis=0)
        o_ref[pl.ds(pl.multiple_of(qi * TQ, TQ), TQ), :] = out_t.T.astype(o_ref.dtype)

    @pl.when(qi == 0)
    def _():
        for h, cols in units:
            put_scores(0, 0, h, cols, rhs)

    def body(i, carry):
        for u in range(ROUND_STAGES):
            stage(ROUND_STAGES * i + u, u % SCORE_SLOTS)
        return carry

    n_rounds = qi // ROUND_STAGES
    lax.fori_loop(0, n_rounds, body, 0)
    base = n_rounds * ROUND_STAGES
    for r in range(ROUND_STAGES):
        @pl.when(qi - base == r)
        def _(r=r):
            for u in range(r):
                stage(base + u, u % SCORE_SLOTS)
            last_stage(qi, r % SCORE_SLOTS)


def _attention(qt, k, vt, cb, batch, seq, fox):
    t = k.shape[0]
    nq = seq // TQ
    assert TQ == TK and TQ % QB == 0 and QB % CHUNK == 0 and ROUND_STAGES % SCORE_SLOTS == 0
    qt_spec = pl.BlockSpec((None, None if fox else 2, LANES, seq), lambda b, p: (b, p, 0, 0))
    in_specs = [
        qt_spec,
        pl.BlockSpec((seq, LANES if fox else 2 * LANES), lambda b, p: (b, p)),
        pl.BlockSpec((None, None, LANES, seq), lambda b, p: (b, p, 0, 0)),
    ]
    args = [qt, k, vt]
    if fox:
        in_specs.append(pl.BlockSpec((seq, LANES), lambda b, p: (b, 0)))
        args.append(cb)
    acc_rows = LANES // 2 + BF16_ROWS
    return pl.pallas_call(
        functools.partial(_attn_kernel, fox=fox, nq=nq),
        grid=(batch, N_PAIRS),
        in_specs=in_specs,
        out_specs=pl.BlockSpec((seq, LANES), lambda b, p: (b, p)),
        out_shape=jax.ShapeDtypeStruct((t, A_WIDTH), BF16),
        scratch_shapes=[pltpu.VMEM((2, 1, TQ), F32), pltpu.VMEM((2, acc_rows, TQ), F32),
                        pltpu.VMEM((SCORE_SLOTS, 2, TK, TQ), F32),
                        pltpu.VMEM((SCORE_SLOTS, 2, 1, TQ), F32)],
        compiler_params=pltpu.CompilerParams(
            dimension_semantics=("arbitrary", "arbitrary"),
            vmem_limit_bytes=VMEM_LIMIT),
        name="fox_attention" if fox else "mla_attention",
    )(*args)


def _post_kernel(x_ref, ya_ref, yb_ref, g_ref, wga_ref, wgb_ref, bga_ref, bgb_ref, wa_ref,
                 wb_ref, wout_ref, gffn_ref, wgate_ref, wup_ref, wdown_ref, gfin_ref,
                 o_ref, act_scr):
    half = x_ref.shape[0] // 2
    ra, rb = slice(0, half), slice(half, 2 * half)

    def norm_in(rows):
        return _rms(x_ref[rows, :], g_ref[...]).astype(BF16)

    def gate(xn, w_ref, b_ref):
        return jax.nn.sigmoid(_dot(xn, w_ref[...]) + b_ref[...])

    def merge_out(rows, gate_a, gate_b):
        merged = (gate_a * _dot(ya_ref[rows, :], wa_ref[...])
                  + gate_b * _dot(yb_ref[rows, :], wb_ref[...]))
        return x_ref[rows, :] + _dot(merged.astype(BF16), wout_ref[...])

    def ffn_chunk(rows, hn, c):
        sl = slice(c, c + FFN_CHUNK)
        gt = _dot(hn, wgate_ref[:, sl])
        act_scr[rows, sl] = (gt * jax.nn.sigmoid(gt) * _dot(hn, wup_ref[:, sl])).astype(BF16)

    chunks = range(0, FFN_HIDDEN, FFN_CHUNK)

    xn_a = norm_in(ra)
    gate_aa = gate(xn_a, wga_ref, bga_ref)
    xn_b = norm_in(rb)
    h_a = merge_out(ra, gate_aa, gate(xn_a, wgb_ref, bgb_ref))
    gate_ab = gate(xn_b, wga_ref, bga_ref)
    hn_a = _rms(h_a, gffn_ref[...]).astype(BF16)
    h_b = merge_out(rb, gate_ab, gate(xn_b, wgb_ref, bgb_ref))
    ffn_chunk(ra, hn_a, chunks[0])
    hn_b = _rms(h_b, gffn_ref[...]).astype(BF16)
    for c in chunks[1:]:
        ffn_chunk(ra, hn_a, c)
    h_a = h_a + _dot(act_scr[ra, :], wdown_ref[...])
    ffn_chunk(rb, hn_b, chunks[0])
    o_ref[ra, :] = _rms(h_a, gfin_ref[...])
    for c in chunks[1:]:
        ffn_chunk(rb, hn_b, c)
    h_b = h_b + _dot(act_scr[rb, :], wdown_ref[...])
    o_ref[rb, :] = _rms(h_b, gfin_ref[...])


def _post(x2, ya, yb, *weights):
    t = x2.shape[0]
    row = lambda w: pl.BlockSpec((TM_OUT, w), lambda i: (i, 0))
    const = lambda w: pl.BlockSpec(w.shape, lambda i: (0,) * w.ndim,
                                   pipeline_mode=pl.Buffered(1))
    return pl.pallas_call(
        _post_kernel,
        grid=(t // TM_OUT,),
        in_specs=[row(D_MODEL), row(A_WIDTH), row(B_WIDTH)] + [const(w) for w in weights],
        out_specs=row(D_MODEL),
        out_shape=jax.ShapeDtypeStruct((t, D_MODEL), F32),
        scratch_shapes=[pltpu.VMEM((TM_OUT, FFN_HIDDEN), BF16)],
        compiler_params=pltpu.CompilerParams(
            dimension_semantics=("arbitrary",), vmem_limit_bytes=VMEM_LIMIT),
        name="merge_ffn",
    )(x2, ya, yb, *weights)


def _mla_lane_layout():
    half = LANES // 2
    nope = np.concatenate([np.arange(HALF_ROPE, half),
                           np.arange(half + HALF_ROPE, B_NOPE + 2 * HALF_ROPE)])
    rope = np.concatenate([np.arange(HALF_ROPE), half + np.arange(HALF_ROPE)])
    return nope, rope


def kernel(x, positions, norm_mix_g, w_in, b_forget, q_a_norm_g, w_q_up, kv_a_norm_g, w_kv_up,
           w_branch_a, w_branch_b, b_gate, w_out, norm_ffn_g, w_ffn_gate, w_ffn_up, w_ffn_down,
           norm_final_g):
    batch, seq, _ = x.shape
    t = batch * seq
    assert norm_mix_g.shape[0] == 1 and seq % TM_IN == 0 and seq % TQ == 0 and t % TM_OUT == 0
    x2 = x.reshape(t, D_MODEL)
    pos3 = positions.reshape(batch, 1, seq)
    nope_lane, rope_lane = _mla_lane_layout()

    w = w_in[0].astype(BF16)
    o = np.cumsum((0, A_WIDTH, A_WIDTH, A_WIDTH, A_HEADS, B_Q_RANK, B_KV_RANK, B_ROPE,
                   D_MODEL, D_MODEL))
    wq, wk, wv = (w[:, o[i]:o[i + 1]] for i in range(3))
    wf = jnp.zeros((D_MODEL, LANES), BF16).at[:, :A_HEADS].set(w[:, o[3]:o[4]])
    wcq, wckv = w[:, o[4]:o[5]], w[:, o[5]:o[6]]
    wkr = jnp.zeros((D_MODEL, LANES), BF16).at[:, rope_lane].set(w[:, o[6]:o[7]])
    wga, wgb = w[:, o[7]:o[8]], w[:, o[8]:o[9]]

    wqu = w_q_up[0].astype(BF16).reshape(B_Q_RANK, B_HEADS, B_NOPE + B_ROPE)
    wqup = jnp.zeros((B_Q_RANK, B_HEADS, B_HEAD_PAD), BF16)
    wqup = wqup.at[:, :, nope_lane].set(wqu[:, :, :B_NOPE]).at[:, :, rope_lane].set(wqu[:, :, B_NOPE:])
    wqup = wqup.reshape(B_Q_RANK, B_HEADS * B_HEAD_PAD)
    wkvu = w_kv_up[0].astype(BF16).reshape(B_KV_RANK, B_HEADS, B_NOPE + B_V_DIM)
    wkup = jnp.zeros((B_KV_RANK, B_HEADS, B_HEAD_PAD), BF16).at[:, :, nope_lane].set(wkvu[:, :, :B_NOPE])
    wkup = wkup.reshape(B_KV_RANK, B_HEADS * B_HEAD_PAD)
    wvup = wkvu[:, :, B_NOPE:].reshape(B_KV_RANK, B_WIDTH)

    invf = (ROPE_BASE ** (-jnp.arange(HALF_ROPE, dtype=F32) / HALF_ROPE)).reshape(HALF_ROPE, 1)
    bf = jnp.broadcast_to(b_forget[0].astype(F32)[:, None], (A_HEADS, TM_IN))
    row = lambda v: v.astype(F32).reshape(1, -1)

    qat, ka, vat, cb, qbt, kb, vbt = _inproj(
        x2, pos3, row(norm_mix_g[0]), wq, wk, wv, wf, wcq, wckv, wkr, bf, invf,
        row(q_a_norm_g[0]), wqup, row(kv_a_norm_g[0]), wkup, wvup, batch, seq)

    ya = _attention(qat, ka, vat, cb, batch, seq, fox=True)
    yb = _attention(qbt, kb, vbt, None, batch, seq, fox=False)

    out = _post(
        x2, ya, yb, row(norm_mix_g[0]), wga, wgb, row(b_gate[0, :D_MODEL]), row(b_gate[0, D_MODEL:]),
        w_branch_a[0].astype(BF16), w_branch_b[0].astype(BF16), w_out[0].astype(BF16),
        row(norm_ffn_g[0]), w_ffn_gate[0].astype(BF16), w_ffn_up[0].astype(BF16),
        w_ffn_down[0].astype(BF16), row(norm_final_g))
    return out.reshape(batch, seq, D_MODEL)
```

```python
import functools

import jax
import jax.numpy as jnp
import numpy as np
from jax import lax
from jax.experimental import pallas as pl
from jax.experimental.pallas import tpu as pltpu

D_MODEL = 1024
CHUNK = 64
A_HEADS = 8
A_HEAD_DIM = 64
A_WIDTH = A_HEADS * A_HEAD_DIM
B_HEADS = 8
B_Q_RANK = 256
B_KV_RANK = 128
B_NOPE = 64
B_ROPE = 32
B_V_DIM = 64
B_WIDTH = B_HEADS * B_V_DIM
ROPE_BASE = 10000.0
FFN_HIDDEN = 2816
NORM_EPS = 1e-6
MASK_VALUE = -1e30

LANES = 128
BF16_ROWS = 16
LOG2E = 1.4426950408889634
A_QSCALE = (A_HEAD_DIM ** -0.5) * LOG2E
B_QSCALE = ((B_NOPE + B_ROPE) ** -0.5) * LOG2E
HALF_ROPE = B_ROPE // 2
B_HEAD_PAD = LANES
N_PAIRS = A_HEADS // 2
BIAS_PIECES = 3

TM_IN = 512
TM_OUT = 512
TQ = 512
TK = 512
SCORE_SLOTS = 3
ROUND_STAGES = 9
QB = 256
FFN_CHUNK = 256
V7X_VMEM_BYTES = 64 * 1024 * 1024
VMEM_LIMIT = V7X_VMEM_BYTES * 7 // 8

F32 = jnp.float32
BF16 = jnp.bfloat16


def _rms(x, g):
    ms = jnp.mean(x * x, axis=-1, keepdims=True)
    return x * lax.rsqrt(ms + NORM_EPS) * g


def _dot(a, b):
    return jnp.dot(a, b, preferred_element_type=F32)


def _rope(t, cos, sin_signed):
    return t * cos + pltpu.roll(t, LANES // 2, axis=1) * sin_signed


def _inproj_kernel(x_ref, pos_ref, g_ref, wq_ref, wk_ref, wv_ref, wf_ref, wcq_ref,
                   wckv_ref, wkr_ref, bf_ref, invf_ref, gq_ref, wqup_ref,
                   gkv_ref, wkup_ref, wvup_ref,
                   qat_ref, ka_ref, vat_ref, cb_ref, qbt_ref, kb_ref, vbt_ref, carry_ref):
    @pl.when(pl.program_id(1) == 0)
    def _():
        carry_ref[...] = jnp.zeros_like(carry_ref)

    tm = x_ref.shape[0]
    xn = _rms(x_ref[...], g_ref[...]).astype(BF16)

    def store_transposed(ref, val, groups):
        for i in range(groups):
            ref[i] = val[:, i * LANES:(i + 1) * LANES].T.astype(BF16)

    f = _dot(xn, wf_ref[...])
    cq = _dot(xn, wcq_ref[...])
    ckv = _dot(xn, wckv_ref[...])
    kr = _dot(xn, wkr_ref[...])
    cqn = _rms(cq, gq_ref[...]).astype(BF16)
    ckvn = _rms(ckv, gkv_ref[...]).astype(BF16)
    qb = _dot(cqn, wqup_ref[...])
    q = _dot(xn, wq_ref[...])

    z = f.T[:A_HEADS] + bf_ref[...]
    c = jnp.minimum(z, 0.0) - jnp.log1p(jnp.exp(-jnp.abs(z)))
    lane = lax.broadcasted_iota(jnp.int32, c.shape, 1)
    d = 1
    while d < tm:
        c = c + jnp.where(lane >= d, pltpu.roll(c, d, axis=1), 0.0)
        d *= 2
    c = c + jnp.tile(carry_ref[...], (1, tm // LANES))
    carry_ref[...] = jnp.broadcast_to(c[:, tm - 1:tm], carry_ref.shape)
    c = c * LOG2E
    hi = c.astype(BF16).astype(F32)
    mid = (c - hi).astype(BF16).astype(F32)
    lo = c - hi - mid
    pad = jnp.zeros((LANES - BIAS_PIECES * A_HEADS, tm), F32)
    cb_ref[...] = jnp.concatenate([hi, mid, lo, pad], axis=0).T.astype(BF16)

    ang = invf_ref[...] * pos_ref[...].astype(F32)
    cos_h, sin_h = jnp.cos(ang), jnp.sin(ang)
    fill = (LANES // 2 - HALF_ROPE, tm)
    cos = jnp.concatenate([cos_h, jnp.ones(fill, F32)] * 2, axis=0).T
    sin_signed = jnp.concatenate(
        [-sin_h, jnp.zeros(fill, F32), sin_h, jnp.zeros(fill, F32)], axis=0).T

    k = _dot(xn, wk_ref[...])
    for h in range(B_HEADS):
        sl = slice(h * B_HEAD_PAD, (h + 1) * B_HEAD_PAD)
        qbt_ref[h] = (_rope(qb[:, sl], cos, sin_signed) * B_QSCALE).T.astype(BF16)
    ka_ref[...] = k.astype(BF16)

    v = _dot(xn, wv_ref[...])
    store_transposed(qat_ref, q * A_QSCALE, N_PAIRS)
    kr = _rope(kr, cos, sin_signed)

    kn = _dot(ckvn, wkup_ref[...])
    store_transposed(vat_ref, v, N_PAIRS)
    vb = _dot(ckvn, wvup_ref[...])
    for h in range(B_HEADS):
        sl = slice(h * B_HEAD_PAD, (h + 1) * B_HEAD_PAD)
        kb_ref[:, sl] = (kn[:, sl] + kr).astype(BF16)
    store_transposed(vbt_ref, vb, N_PAIRS)


def _full(shape):
    return pl.BlockSpec(shape, lambda *_: (0,) * len(shape))


def _inproj(x2, pos3, g, wq, wk, wv, wf, wcq, wckv, wkr, bf, invf, gq, wqup, gkv,
            wkup, wvup, batch, seq):
    t = x2.shape[0]
    nt = seq // TM_IN
    row = lambda w: pl.BlockSpec((TM_IN, w), lambda b, s: (b * nt + s, 0))
    col = lambda n: pl.BlockSpec((None, n, LANES, TM_IN), lambda b, s: (b, 0, 0, s))
    col_shape = lambda n: jax.ShapeDtypeStruct((batch, n, LANES, seq), BF16)
    pos_spec = pl.BlockSpec((None, 1, TM_IN), lambda b, s: (b, 0, s))
    weights = (g, wq, wk, wv, wf, wcq, wckv, wkr, bf, invf, gq, wqup, gkv, wkup, wvup)
    return pl.pallas_call(
        _inproj_kernel,
        grid=(batch, nt),
        in_specs=[row(D_MODEL), pos_spec] + [_full(w.shape) for w in weights],
        out_specs=[
            col(N_PAIRS), row(A_WIDTH), col(N_PAIRS), row(LANES),
            col(B_HEADS), row(B_HEADS * B_HEAD_PAD), col(N_PAIRS),
        ],
        out_shape=[
            col_shape(N_PAIRS),
            jax.ShapeDtypeStruct((t, A_WIDTH), BF16),
            col_shape(N_PAIRS),
            jax.ShapeDtypeStruct((t, LANES), BF16),
            col_shape(B_HEADS),
            jax.ShapeDtypeStruct((t, B_HEADS * B_HEAD_PAD), BF16),
            col_shape(N_PAIRS),
        ],
        scratch_shapes=[pltpu.VMEM((A_HEADS, LANES), F32)],
        compiler_params=pltpu.CompilerParams(
            dimension_semantics=("arbitrary", "arbitrary"), vmem_limit_bytes=VMEM_LIMIT),
        name="inproj",
    )(x2, pos3, *weights)


def _attn_kernel(*refs, fox, nq):
    def tile(qi, carry):
        _attn_tile(qi, refs, fox, nq)
        return carry

    lax.fori_loop(0, nq, tile, 0)


def _attn_tile(qi, refs, fox, nq):
    if fox:
        qt_ref, k_ref, vt_ref, cb_ref, o_ref, m_scr, acc_scr, s_scr, smax_scr = refs
    else:
        qt_ref, k_ref, vt_ref, o_ref, m_scr, acc_scr, s_scr, smax_scr = refs
    head_dim = LANES // 2
    if fox:
        frow = lax.broadcasted_iota(jnp.int32, (LANES, TQ), 0)
        pair = pl.program_id(1)
        mine = [(frow < head_dim) == (h == 0) for h in range(2)]
        picks = [jnp.where((frow % A_HEADS == 2 * pair + h) & (frow < BIAS_PIECES * A_HEADS),
                           -1.0, 0.0).astype(BF16) for h in range(2)]

    def query_operands(tile):
        q0 = pl.multiple_of(tile * TQ, TQ)
        if fox:
            qt = qt_ref[:, pl.ds(q0, TQ)]
            return [jnp.concatenate([jnp.where(mine[h], qt, jnp.zeros_like(qt)), picks[h]], axis=0)
                    for h in range(2)]
        return [qt_ref[h, :, pl.ds(q0, TQ)] for h in range(2)]

    rhs = query_operands(qi)
    rhs_next = query_operands(jnp.minimum(qi + 1, nq - 1))

    m_scr[...] = jnp.full_like(m_scr, MASK_VALUE)
    acc_scr[...] = jnp.zeros_like(acc_scr)

    keys = lax.broadcasted_iota(jnp.int32, (TK, TQ), 0)
    queries = lax.broadcasted_iota(jnp.int32, (TK, TQ), 1)
    if fox:
        diag_mask = keys <= queries
    else:
        diag_mask = (keys // CHUNK) <= (queries // CHUNK)
    ones = jnp.ones((BF16_ROWS, TK), BF16)

    units = [(h, slice(c, c + QB)) for h in range(2) for c in range(0, TQ, QB)]

    def scores(j, h, cols, operands):
        start = pl.multiple_of(j * TK, TK)
        if fox:
            lhs = jnp.concatenate([k_ref[pl.ds(start, TK), :], cb_ref[pl.ds(start, TK), :]], axis=1)
        else:
            lhs = k_ref[pl.ds(start, TK), h * LANES:(h + 1) * LANES]
        return _dot(lhs, operands[h][:, cols])

    def put_scores(slot, j, h, cols, operands):
        s = scores(j, h, cols, operands)
        s_scr[slot, h, :, cols] = s
        smax_scr[slot, h, :, cols] = jnp.max(s, axis=0, keepdims=True)

    def update(j, slot, h, cols, masked):
        start = pl.multiple_of(j * TK, TK)
        nk = cols.stop if masked else TK
        s = s_scr[slot, h, :nk, cols]
        if masked:
            s = jnp.where(diag_mask[:nk, cols], s, MASK_VALUE)
            s_max = jnp.max(s, axis=0, keepdims=True)
        else:
            s_max = smax_scr[slot, h, :, cols]
        m_prev = m_scr[h, :, cols]
        m_new = jnp.maximum(m_prev, s_max)
        alpha = jnp.exp2(m_prev - m_new)
        p = jnp.exp2(s - m_new).astype(BF16)
        vt = jnp.concatenate(
            [vt_ref[h * head_dim:(h + 1) * head_dim, pl.ds(start, nk)], ones[:, :nk]], axis=0)
        acc_scr[h, :, cols] = alpha * acc_scr[h, :, cols] + _dot(vt, p)
        m_scr[h, :, cols] = m_new

    def stage(j, slot):
        nxt = (slot + 1) % SCORE_SLOTS
        for i in range(len(units) + 1):
            if i < len(units):
                h, cols = units[i]
                put_scores(nxt, j + 1, h, cols, rhs)
            if i > 0:
                update(j, slot, *units[i - 1], False)

    def last_stage(j, slot):
        for h, cols in units:
            final = (h, cols) == units[-1]
            if slot == 0 or final:
                update(j, slot, h, cols, True)
            if final:
                finish()
            put_scores(0, 0, h, cols, rhs_next)
            if slot != 0 and not final:
                update(j, slot, h, cols, True)

    def finish():
        out_t = jnp.concatenate(
            [acc_scr[h, :head_dim] / acc_scr[h, head_dim:head_dim + 1] for h in range(2)], axis=0)
        o_ref[pl.ds(pl.multiple_of(qi * TQ, TQ), TQ), :] = out_t.T.astype(o_ref.dtype)

    @pl.when(qi == 0)
    def _():
        for h, cols in units:
            put_scores(0, 0, h, cols, rhs)

    def body(i, carry):
        for u in range(ROUND_STAGES):
            stage(ROUND_STAGES * i + u, u % SCORE_SLOTS)
        return carry

    n_rounds = qi // ROUND_STAGES
    lax.fori_loop(0, n_rounds, body, 0)
    base = n_rounds * ROUND_STAGES
    for r in range(ROUND_STAGES):
        @pl.when(qi - base == r)
        def _(r=r):
            for u in range(r):
                stage(base + u, u % SCORE_SLOTS)
            last_stage(qi, r % SCORE_SLOTS)


def _attention(qt, k, vt, cb, batch, seq, fox):
    t = k.shape[0]
    nq = seq // TQ
    assert TQ == TK and TQ % QB == 0 and QB % CHUNK == 0 and ROUND_STAGES % SCORE_SLOTS == 0
    qt_spec = pl.BlockSpec((None, None if fox else 2, LANES, seq), lambda b, p: (b, p, 0, 0))
    in_specs = [
        qt_spec,
        pl.BlockSpec((seq, LANES if fox else 2 * LANES), lambda b, p: (b, p)),
        pl.BlockSpec((None, None, LANES, seq), lambda b, p: (b, p, 0, 0)),
    ]
    args = [qt, k, vt]
    if fox:
        in_specs.append(pl.BlockSpec((seq, LANES), lambda b, p: (b, 0)))
        args.append(cb)
    acc_rows = LANES // 2 + BF16_ROWS
    return pl.pallas_call(
        functools.partial(_attn_kernel, fox=fox, nq=nq),
        grid=(batch, N_PAIRS),
        in_specs=in_specs,
        out_specs=pl.BlockSpec((seq, LANES), lambda b, p: (b, p)),
        out_shape=jax.ShapeDtypeStruct((t, A_WIDTH), BF16),
        scratch_shapes=[pltpu.VMEM((2, 1, TQ), F32), pltpu.VMEM((2, acc_rows, TQ), F32),
                        pltpu.VMEM((SCORE_SLOTS, 2, TK, TQ), F32),
                        pltpu.VMEM((SCORE_SLOTS, 2, 1, TQ), F32)],
        compiler_params=pltpu.CompilerParams(
            dimension_semantics=("arbitrary", "arbitrary"),
            vmem_limit_bytes=VMEM_LIMIT),
        name="fox_attention" if fox else "mla_attention",
    )(*args)


def _post_kernel(x_ref, ya_ref, yb_ref, g_ref, wga_ref, wgb_ref, bga_ref, bgb_ref, wa_ref,
                 wb_ref, wout_ref, gffn_ref, wgate_ref, wup_ref, wdown_ref, gfin_ref,
                 o_ref, act_scr):
    half = x_ref.shape[0] // 2
    ra, rb = slice(0, half), slice(half, 2 * half)

    def norm_in(rows):
        return _rms(x_ref[rows, :], g_ref[...]).astype(BF16)

    def gate(xn, w_ref, b_ref):
        return jax.nn.sigmoid(_dot(xn, w_ref[...]) + b_ref[...])

    def merge_out(rows, gate_a, gate_b):
        merged = (gate_a * _dot(ya_ref[rows, :], wa_ref[...])
                  + gate_b * _dot(yb_ref[rows, :], wb_ref[...]))
        return x_ref[rows, :] + _dot(merged.astype(BF16), wout_ref[...])

    def ffn_chunk(rows, hn, c):
        sl = slice(c, c + FFN_CHUNK)
        gt = _dot(hn, wgate_ref[:, sl])
        act_scr[rows, sl] = (gt * jax.nn.sigmoid(gt) * _dot(hn, wup_ref[:, sl])).astype(BF16)

    chunks = range(0, FFN_HIDDEN, FFN_CHUNK)

    xn_a = norm_in(ra)
    gate_aa = gate(xn_a, wga_ref, bga_ref)
    xn_b = norm_in(rb)
    h_a = merge_out(ra, gate_aa, gate(xn_a, wgb_ref, bgb_ref))
    gate_ab = gate(xn_b, wga_ref, bga_ref)
    hn_a = _rms(h_a, gffn_ref[...]).astype(BF16)
    h_b = merge_out(rb, gate_ab, gate(xn_b, wgb_ref, bgb_ref))
    ffn_chunk(ra, hn_a, chunks[0])
    hn_b = _rms(h_b, gffn_ref[...]).astype(BF16)
    for c in chunks[1:]:
        ffn_chunk(ra, hn_a, c)
    h_a = h_a + _dot(act_scr[ra, :], wdown_ref[...])
    ffn_chunk(rb, hn_b, chunks[0])
    o_ref[ra, :] = _rms(h_a, gfin_ref[...])
    for c in chunks[1:]:
        ffn_chunk(rb, hn_b, c)
    h_b = h_b + _dot(act_scr[rb, :], wdown_ref[...])
    o_ref[rb, :] = _rms(h_b, gfin_ref[...])


def _post(x2, ya, yb, *weights):
    t = x2.shape[0]
    row = lambda w: pl.BlockSpec((TM_OUT, w), lambda i: (i, 0))
    const = lambda w: pl.BlockSpec(w.shape, lambda i: (0,) * w.ndim,
                                   pipeline_mode=pl.Buffered(1))
    return pl.pallas_call(
        _post_kernel,
        grid=(t // TM_OUT,),
        in_specs=[row(D_MODEL), row(A_WIDTH), row(B_WIDTH)] + [const(w) for w in weights],
        out_specs=row(D_MODEL),
        out_shape=jax.ShapeDtypeStruct((t, D_MODEL), F32),
        scratch_shapes=[pltpu.VMEM((TM_OUT, FFN_HIDDEN), BF16)],
        compiler_params=pltpu.CompilerParams(
            dimension_semantics=("arbitrary",), vmem_limit_bytes=VMEM_LIMIT),
        name="merge_ffn",
    )(x2, ya, yb, *weights)


def _mla_lane_layout():
    half = LANES // 2
    nope = np.concatenate([np.arange(HALF_ROPE, half),
                           np.arange(half + HALF_ROPE, B_NOPE + 2 * HALF_ROPE)])
    rope = np.concatenate([np.arange(HALF_ROPE), half + np.arange(HALF_ROPE)])
    return nope, rope


def kernel(x, positions, norm_mix_g, w_in, b_forget, q_a_norm_g, w_q_up, kv_a_norm_g, w_kv_up,
           w_branch_a, w_branch_b, b_gate, w_out, norm_ffn_g, w_ffn_gate, w_ffn_up, w_ffn_down,
           norm_final_g):
    batch, seq, _ = x.shape
    t = batch * seq
    assert norm_mix_g.shape[0] == 1 and seq % TM_IN == 0 and seq % TQ == 0 and t % TM_OUT == 0
    x2 = x.reshape(t, D_MODEL)
    pos3 = positions.reshape(batch, 1, seq)
    nope_lane, rope_lane = _mla_lane_layout()

    w = w_in[0].astype(BF16)
    o = np.cumsum((0, A_WIDTH, A_WIDTH, A_WIDTH, A_HEADS, B_Q_RANK, B_KV_RANK, B_ROPE,
                   D_MODEL, D_MODEL))
    wq, wk, wv = (w[:, o[i]:o[i + 1]] for i in range(3))
    wf = jnp.zeros((D_MODEL, LANES), BF16).at[:, :A_HEADS].set(w[:, o[3]:o[4]])
    wcq, wckv = w[:, o[4]:o[5]], w[:, o[5]:o[6]]
    wkr = jnp.zeros((D_MODEL, LANES), BF16).at[:, rope_lane].set(w[:, o[6]:o[7]])
    wga, wgb = w[:, o[7]:o[8]], w[:, o[8]:o[9]]

    wqu = w_q_up[0].astype(BF16).reshape(B_Q_RANK, B_HEADS, B_NOPE + B_ROPE)
    wqup = jnp.zeros((B_Q_RANK, B_HEADS, B_HEAD_PAD), BF16)
    wqup = wqup.at[:, :, nope_lane].set(wqu[:, :, :B_NOPE]).at[:, :, rope_lane].set(wqu[:, :, B_NOPE:])
    wqup = wqup.reshape(B_Q_RANK, B_HEADS * B_HEAD_PAD)
    wkvu = w_kv_up[0].astype(BF16).reshape(B_KV_RANK, B_HEADS, B_NOPE + B_V_DIM)
    wkup = jnp.zeros((B_KV_RANK, B_HEADS, B_HEAD_PAD), BF16).at[:, :, nope_lane].set(wkvu[:, :, :B_NOPE])
    wkup = wkup.reshape(B_KV_RANK, B_HEADS * B_HEAD_PAD)
    wvup = wkvu[:, :, B_NOPE:].reshape(B_KV_RANK, B_WIDTH)

    invf = (ROPE_BASE ** (-jnp.arange(HALF_ROPE, dtype=F32) / HALF_ROPE)).reshape(HALF_ROPE, 1)
    bf = jnp.broadcast_to(b_forget[0].astype(F32)[:, None], (A_HEADS, TM_IN))
    row = lambda v: v.astype(F32).reshape(1, -1)

    qat, ka, vat, cb, qbt, kb, vbt = _inproj(
        x2, pos3, row(norm_mix_g[0]), wq, wk, wv, wf, wcq, wckv, wkr, bf, invf,
        row(q_a_norm_g[0]), wqup, row(kv_a_norm_g[0]), wkup, wvup, batch, seq)

    ya = _attention(qat, ka, vat, cb, batch, seq, fox=True)
    yb = _attention(qbt, kb, vbt, None, batch, seq, fox=False)

    out = _post(
        x2, ya, yb, row(norm_mix_g[0]), wga, wgb, row(b_gate[0, :D_MODEL]), row(b_gate[0, D_MODEL:]),
        w_branch_a[0].astype(BF16), w_branch_b[0].astype(BF16), w_out[0].astype(BF16),
        row(norm_ffn_g[0]), w_ffn_gate[0].astype(BF16), w_ffn_up[0].astype(BF16),
        w_ffn_down[0].astype(BF16), row(norm_final_g))
    return out.reshape(batch, seq, D_MODEL)
```
